```python
import math
import jax, jax.numpy as jnp
from jax import lax
import numpy as np

D_MODEL = 1024
BATCH = 16
SEQ = 4096
DEPTH = 1

CHUNK = 64
Q_BLOCK = 128

FOX_HEADS = 8
FOX_HEAD_DIM = 64
FOX_WIDTH = FOX_HEADS * FOX_HEAD_DIM

DIFF_HEADS = 4
DIFF_HALF_DIM = 64
DIFF_HEAD_DIM = 2 * DIFF_HALF_DIM
DIFF_WIDTH = DIFF_HEADS * DIFF_HEAD_DIM

MIX_WIDTH = FOX_WIDTH + DIFF_WIDTH
IN_COLS = 3 * FOX_WIDTH + FOX_HEADS + 3 * DIFF_WIDTH

PEER_HEADS = 8
PEER_N_KEYS = 128
PEER_N_EXPERTS = PEER_N_KEYS * PEER_N_KEYS
PEER_HALF_DIM = 128
PEER_QUERY_DIM = 2 * PEER_HALF_DIM
PEER_TOPK = 16
PEER_TOKEN_BLOCK = 128

RMS_EPS = 1e-6

kernel_name = "hybrid_fox_diffattn_peer_block"


def _rmsnorm(x, g):
    xf = x.astype(jnp.float32)
    y = xf * lax.rsqrt(jnp.mean(xf * xf, axis=-1, keepdims=True) + RMS_EPS)
    return (y * g.astype(jnp.float32)).astype(x.dtype)


def _alibi_slopes(n):
    return jnp.asarray([2.0 ** (-8.0 * (i + 1) / n) for i in range(n)], dtype=jnp.float32)


def _sweep(body, n_blocks):
    out = lax.map(body, jnp.arange(n_blocks))
    nb, b, h, qb, dv = out.shape
    return out.transpose(1, 0, 3, 2, 4).reshape(b, nb * qb, h * dv)


def _forgetting_attention(q, k, v, log_f):
    seq = q.shape[2]
    scale = FOX_HEAD_DIM ** -0.5
    c = jnp.cumsum(log_f, axis=-1)
    key_pos = jnp.arange(seq)

    def body(i):
        start = i * Q_BLOCK
        qb = lax.dynamic_slice_in_dim(q, start, Q_BLOCK, axis=2)
        cq = lax.dynamic_slice_in_dim(c, start, Q_BLOCK, axis=2)
        t = start + jnp.arange(Q_BLOCK)
        logits = (jnp.einsum('bhqd,bhkd->bhqk', qb, k).astype(jnp.float32) * scale
                  + (cq[..., :, None] - c[..., None, :]))
        mask = key_pos[None, :] <= t[:, None]
        p = jax.nn.softmax(jnp.where(mask, logits, -jnp.inf), axis=-1)
        return jnp.einsum('bhqk,bhkd->bhqd', p.astype(v.dtype), v)

    return _sweep(body, seq // Q_BLOCK)


def _differential_attention(q1, k1, q2, k2, v, lam, lambda_init, subln_g):
    seq = q1.shape[2]
    scale = DIFF_HALF_DIM ** -0.5
    slopes = _alibi_slopes(DIFF_HEADS)
    key_pos = jnp.arange(seq)

    def body(i):
        start = i * Q_BLOCK
        q1b = lax.dynamic_slice_in_dim(q1, start, Q_BLOCK, axis=2)
        q2b = lax.dynamic_slice_in_dim(q2, start, Q_BLOCK, axis=2)
        t = start + jnp.arange(Q_BLOCK)
        dist = jnp.abs(t[:, None] - key_pos[None, :]).astype(jnp.float32)
        alibi = -slopes[:, None, None] * dist[None]
        mask = (key_pos[None, :] // CHUNK) <= (t[:, None] // CHUNK)
        l1 = jnp.einsum('bhqd,bhkd->bhqk', q1b, k1).astype(jnp.float32) * scale + alibi
        l2 = jnp.einsum('bhqd,bhkd->bhqk', q2b, k2).astype(jnp.float32) * scale + alibi
        a1 = jax.nn.softmax(jnp.where(mask, l1, -jnp.inf), axis=-1)
        a2 = jax.nn.softmax(jnp.where(mask, l2, -jnp.inf), axis=-1)
        attn = a1 - lam * a2
        o = jnp.einsum('bhqk,bhkd->bhqd', attn.astype(v.dtype), v)
        return _rmsnorm(o, subln_g) * (1.0 - lambda_init)

    return _sweep(body, seq // Q_BLOCK)


def _peer(xn, w_q, sub_keys, u_tab, v_tab):
    b, s, d = xn.shape
    blocks = xn.reshape(-1, PEER_TOKEN_BLOCK, d)

    def body(xb):
        q = (xb @ w_q).reshape(PEER_TOKEN_BLOCK, PEER_HEADS, 2, PEER_HALF_DIM)
        sc = jnp.einsum('thpd,hpkd->thpk', q, sub_keys).astype(jnp.float32)
        s1, i1 = lax.top_k(sc[:, :, 0], PEER_TOPK)
        s2, i2 = lax.top_k(sc[:, :, 1], PEER_TOPK)
        cand = (s1[..., :, None] + s2[..., None, :]).reshape(PEER_TOKEN_BLOCK, PEER_HEADS, -1)
        cidx = (i1[..., :, None] * PEER_N_KEYS + i2[..., None, :]).reshape(PEER_TOKEN_BLOCK, PEER_HEADS, -1)
        top_s, pos = lax.top_k(cand, PEER_TOPK)
        eidx = jnp.take_along_axis(cidx, pos, axis=-1)
        g = jax.nn.softmax(top_s, axis=-1).astype(xb.dtype)
        u_sel = jnp.take(u_tab, eidx, axis=0)
        hid = jax.nn.gelu(jnp.einsum('thkd,td->thk', u_sel, xb), approximate=False) * g
        v_sel = jnp.take(v_tab, eidx, axis=0)
        return jnp.einsum('thk,thkd->td', hid, v_sel)

    return lax.map(body, blocks).reshape(b, s, d)


def setup_inputs(seed: int = 0) -> dict:
    key = jax.random.key(seed)
    ks = jax.random.split(key, 17)
    f32 = jnp.float32
    n = lambda k, shape, sc: jax.random.normal(k, shape, f32) * sc
    return {
        "x": n(ks[0], (BATCH, SEQ, D_MODEL), 1.0),
        "norm1_g": 1.0 + n(ks[1], (DEPTH, D_MODEL), 0.02),
        "w_in": n(ks[2], (DEPTH, D_MODEL, IN_COLS), D_MODEL ** -0.5),
        "b_f": 2.5 + n(ks[3], (DEPTH, FOX_HEADS), 0.5),
        "lambda_q1": n(ks[4], (DEPTH, DIFF_HALF_DIM), 0.1),
        "lambda_k1": n(ks[5], (DEPTH, DIFF_HALF_DIM), 0.1),
        "lambda_q2": n(ks[6], (DEPTH, DIFF_HALF_DIM), 0.1),
        "lambda_k2": n(ks[7], (DEPTH, DIFF_HALF_DIM), 0.1),
        "subln_g": 1.0 + n(ks[8], (DEPTH, DIFF_HEAD_DIM), 0.02),
        "w_out": n(ks[9], (DEPTH, MIX_WIDTH, D_MODEL), MIX_WIDTH ** -0.5),
        "norm2_g": 1.0 + n(ks[10], (DEPTH, D_MODEL), 0.02),
        "peer_w_q": n(ks[11], (DEPTH, D_MODEL, PEER_HEADS * PEER_QUERY_DIM), D_MODEL ** -0.5),
        "peer_sub_keys": n(ks[12], (DEPTH, PEER_HEADS, 2, PEER_N_KEYS, PEER_HALF_DIM), PEER_HALF_DIM ** -0.5),
        "peer_u": n(ks[13], (DEPTH, PEER_N_EXPERTS, D_MODEL), D_MODEL ** -0.5),
        "peer_v": n(ks[14], (DEPTH, PEER_N_EXPERTS, D_MODEL), 0.5),
        "final_g": 1.0 + n(ks[15], (D_MODEL,), 0.02),
    }


def reference(x, norm1_g, w_in, b_f, lambda_q1, lambda_k1, lambda_q2, lambda_k2,
              subln_g, w_out, norm2_g, peer_w_q, peer_sub_keys, peer_u, peer_v, final_g):
    b, s, _ = x.shape
    split_pts = np.cumsum([FOX_WIDTH, FOX_WIDTH, FOX_WIDTH, FOX_HEADS, DIFF_WIDTH, DIFF_WIDTH])
    h = x
    for layer in range(DEPTH):
        xn = _rmsnorm(h, norm1_g[layer])
        proj = xn @ w_in[layer]
        qa, ka, va, fa, qb, kb, vb = jnp.split(proj, split_pts, axis=-1)

        heads_a = lambda t: t.reshape(b, s, FOX_HEADS, FOX_HEAD_DIM).transpose(0, 2, 1, 3)
        log_f = jax.nn.log_sigmoid((fa + b_f[layer]).astype(jnp.float32)).transpose(0, 2, 1)
        out_a = _forgetting_attention(heads_a(qa), heads_a(ka), heads_a(va), log_f)

        heads_b = lambda t: t.reshape(b, s, DIFF_HEADS, DIFF_HEAD_DIM).transpose(0, 2, 1, 3)
        qb_h, kb_h, vb_h = heads_b(qb), heads_b(kb), heads_b(vb)
        lambda_init = 0.8 - 0.6 * math.exp(-0.3 * layer)
        lam = (jnp.exp(jnp.sum(lambda_q1[layer].astype(jnp.float32) * lambda_k1[layer].astype(jnp.float32)))
               - jnp.exp(jnp.sum(lambda_q2[layer].astype(jnp.float32) * lambda_k2[layer].astype(jnp.float32)))
               + lambda_init)
        out_b = _differential_attention(
            qb_h[..., :DIFF_HALF_DIM], kb_h[..., :DIFF_HALF_DIM],
            qb_h[..., DIFF_HALF_DIM:], kb_h[..., DIFF_HALF_DIM:],
            vb_h, lam, lambda_init, subln_g[layer])

        mixed = jnp.concatenate([out_a, out_b], axis=-1)
        h = h + mixed @ w_out[layer]

        hn = _rmsnorm(h, norm2_g[layer])
        h = h + _peer(hn, peer_w_q[layer], peer_sub_keys[layer], peer_u[layer], peer_v[layer])
    return _rmsnorm(h, final_g)
```

```python
import functools
import math

import jax
import jax.numpy as jnp
from jax import lax
from jax.experimental import pallas as pl
from jax.experimental.pallas import tpu as pltpu

F32 = jnp.float32
BF16 = jnp.bfloat16

D_MODEL = 1024
FOX_HEADS = 8
FOX_HEAD_DIM = 64
FOX_WIDTH = FOX_HEADS * FOX_HEAD_DIM
DIFF_HEADS = 4
DIFF_HALF_DIM = 64
DIFF_HEAD_DIM = 2 * DIFF_HALF_DIM
DIFF_WIDTH = DIFF_HEADS * DIFF_HEAD_DIM
CHUNK = 64
PEER_HEADS = 8
PEER_N_KEYS = 128
PEER_N_EXPERTS = PEER_N_KEYS * PEER_N_KEYS
PEER_HALF_DIM = 128
PEER_TOPK = 16
RMS_EPS = 1e-6

LANES = 128
SUBLANES = 8
NEG_BIG = -1e30
VMEM_LIMIT = 56 * 1024 * 1024

GATE_GROUPS = (0, 8, 16, 32, 40, 48)

NT_DIMS = (((1,), (1,)), ((), ()))


def _split3(v):
    hi = v.astype(BF16)
    r1 = v - hi.astype(F32)
    mid = r1.astype(BF16)
    lo = (r1 - mid.astype(F32)).astype(BF16)
    return hi, mid, lo


def _inproj_kernel(x_ref, g_ref, w_ref, bf_ref, tril_ref,
                   qkva_ref, bq_ref, bk_ref, qkvb_ref, carry_ref):
    tm = x_ref.shape[0]

    @pl.when(pl.program_id(1) == 0)
    def _():
        carry_ref[...] = jnp.zeros_like(carry_ref)

    x = x_ref[...]
    ms = jnp.mean(x * x, axis=-1, keepdims=True)
    xn = x * lax.rsqrt(ms + RMS_EPS) * g_ref[...]
    proj = jnp.dot(xn.astype(BF16), w_ref[...], preferred_element_type=F32)

    wa = 3 * FOX_WIDTH
    wb = 3 * DIFF_WIDTH
    qkva_ref[:, 0:FOX_WIDTH] = (proj[:, 0:FOX_WIDTH] * 0.125).astype(BF16)
    qkva_ref[:, FOX_WIDTH:wa] = proj[:, FOX_WIDTH:wa].astype(BF16)
    qkvb_ref[:, 0:DIFF_WIDTH] = (proj[:, wa:wa + DIFF_WIDTH] * 0.125).astype(BF16)
    qkvb_ref[:, DIFF_WIDTH:wb] = proj[:, wa + DIFF_WIDTH:wa + wb].astype(BF16)

    z = proj[:, wa + wb:wa + wb + LANES] + bf_ref[...]
    logf = jnp.minimum(z, 0.0) - jnp.log1p(jnp.exp(-jnp.abs(z)))
    tril = tril_ref[...]
    hi, mid, lo = _split3(logf)
    c = (jnp.dot(tril, hi, preferred_element_type=F32)
         + jnp.dot(tril, mid, preferred_element_type=F32)
         + jnp.dot(tril, lo, preferred_element_type=F32)) + carry_ref[0:1, :]
    carry_ref[...] = jnp.broadcast_to(c[tm - 1:tm, :], carry_ref.shape)

    chi, cmid, clo = _split3(c)
    chi, cmid, clo = chi.astype(F32), cmid.astype(F32), clo.astype(F32)
    lane = lax.broadcasted_iota(jnp.int32, c.shape, 1)
    ones_hi = (lane >= 32) & (lane < 56)
    bq = jnp.where(lane < 8, chi, jnp.where(lane < 16, cmid, jnp.where(lane < 24, clo,
                   jnp.where(ones_hi, 1.0, 0.0))))
    bk = jnp.where(lane < 24, 1.0, jnp.where(lane < 32, 0.0, jnp.where(lane < 40, -chi,
                   jnp.where(lane < 48, -cmid, jnp.where(lane < 56, -clo, 0.0)))))
    bq_ref[...] = bq.astype(BF16)
    bk_ref[...] = bk.astype(BF16)


def _inproj(x, g, w_all, bf128, tm):
    b, s, d = x.shape
    ncol = w_all.shape[1]
    tril = jnp.tril(jnp.ones((tm, tm), F32)).astype(BF16)
    row = lambda bi, si: (bi, si, 0)
    const = lambda bi, si: (0, 0)
    return pl.pallas_call(
        _inproj_kernel,
        grid=(b, s // tm),
        in_specs=[
            pl.BlockSpec((None, tm, d), row),
            pl.BlockSpec((1, d), const),
            pl.BlockSpec((d, ncol), const),
            pl.BlockSpec((1, LANES), const),
            pl.BlockSpec((tm, tm), const),
        ],
        out_specs=[
            pl.BlockSpec((None, tm, 3 * FOX_WIDTH), row),
            pl.BlockSpec((None, tm, LANES), row),
            pl.BlockSpec((None, tm, LANES), row),
            pl.BlockSpec((None, tm, 3 * DIFF_WIDTH), row),
        ],
        out_shape=[
            jax.ShapeDtypeStruct((b, s, 3 * FOX_WIDTH), BF16),
            jax.ShapeDtypeStruct((b, s, LANES), BF16),
            jax.ShapeDtypeStruct((b, s, LANES), BF16),
            jax.ShapeDtypeStruct((b, s, 3 * DIFF_WIDTH), BF16),
        ],
        scratch_shapes=[pltpu.VMEM((SUBLANES, LANES), F32)],
        compiler_params=pltpu.CompilerParams(
            dimension_semantics=("arbitrary", "arbitrary"), vmem_limit_bytes=VMEM_LIMIT),
        name="inproj",
    )(x, g, w_all, bf128, tril)


def _online_softmax_step(s, m, l, acc, v):
    m_new = jnp.maximum(m, jnp.max(s, axis=-1, keepdims=True))
    p = jnp.exp(s - m_new)
    alpha = jnp.exp(m - m_new)
    l_new = alpha * l + jnp.sum(p, axis=-1, keepdims=True)
    acc_new = alpha * acc + jnp.dot(p.astype(BF16), v, preferred_element_type=F32)
    return m_new, l_new, acc_new


def _fox_kernel(q_ref, k_ref, v_ref, bq_ref, bk_ref, o_ref):
    tq = q_ref.shape[0]
    pair = pl.program_id(1)
    qi = pl.program_id(2)
    lane = lax.broadcasted_iota(jnp.int32, (tq, LANES), 1)
    qf = q_ref[...].astype(F32)
    bqf = bq_ref[...].astype(F32)

    qa = []
    for hh in range(2):
        head = 2 * pair + hh
        qm = jnp.where((lane >= FOX_HEAD_DIM * hh) & (lane < FOX_HEAD_DIM * (hh + 1)), qf, 0.0)
        bm = jnp.where(((lane & 7) == head) & (lane < 56), bqf, 0.0)
        qa.append(jnp.concatenate([qm, bm], axis=1).astype(BF16))

    def load_kv(j):
        ks = pl.multiple_of(j * tq, tq)
        ka = jnp.concatenate([k_ref[pl.ds(ks, tq), :], bk_ref[pl.ds(ks, tq), :]], axis=1)
        return ka, v_ref[pl.ds(ks, tq), :]

    def step(j, carry, causal):
        ka, v = load_kv(j)
        out = []
        for hh in range(2):
            m, l, acc = carry[hh]
            s = lax.dot_general(qa[hh], ka, NT_DIMS, preferred_element_type=F32)
            if causal:
                row = lax.broadcasted_iota(jnp.int32, s.shape, 0)
                col = lax.broadcasted_iota(jnp.int32, s.shape, 1)
                s = jnp.where(col <= row, s, -jnp.inf)
            out.append(_online_softmax_step(s, m, l, acc, v))
        return tuple(out)

    init = tuple((jnp.full((tq, 1), NEG_BIG, F32), jnp.zeros((tq, 1), F32),
                  jnp.zeros((tq, LANES), F32)) for _ in range(2))
    carry = lax.fori_loop(0, qi, lambda j, c: step(j, c, False), init)
    carry = step(qi, carry, True)
    o0 = carry[0][2] / carry[0][1]
    o1 = carry[1][2] / carry[1][1]
    o_ref[...] = jnp.where(lane < FOX_HEAD_DIM, o0, o1).astype(BF16)


def _fox(qkva, bq, bk, tq):
    b, s, _ = qkva.shape
    npair = FOX_HEADS // 2
    return pl.pallas_call(
        _fox_kernel,
        grid=(b, npair, s // tq),
        in_specs=[
            pl.BlockSpec((None, tq, LANES), lambda bi, p, i: (bi, i, p)),
            pl.BlockSpec((None, s, LANES), lambda bi, p, i: (bi, 0, npair + p)),
            pl.BlockSpec((None, s, LANES), lambda bi, p, i: (bi, 0, 2 * npair + p)),
            pl.BlockSpec((None, tq, LANES), lambda bi, p, i: (bi, i, 0)),
            pl.BlockSpec((None, s, LANES), lambda bi, p, i: (bi, 0, 0)),
        ],
        out_specs=pl.BlockSpec((None, tq, LANES), lambda bi, p, i: (bi, i, p)),
        out_shape=jax.ShapeDtypeStruct((b, s, FOX_WIDTH), BF16),
        compiler_params=pltpu.CompilerParams(
            dimension_semantics=("parallel", "parallel", "arbitrary"), vmem_limit_bytes=VMEM_LIMIT),
        name="fox",
    )(qkva, qkva, qkva, bq, bk)


def _diff_kernel(q_ref, k_ref, v_ref, pq_ref, pk_ref, slope_ref, lam_ref, g_ref, o_ref, *, out_scale,
                 lambda_init):
    tq = q_ref.shape[0]
    qi = pl.program_id(2)
    lane = lax.broadcasted_iota(jnp.int32, (tq, LANES), 1)
    qf = q_ref[...].astype(F32)
    q1 = jnp.where(lane < DIFF_HALF_DIM, qf, 0.0).astype(BF16)
    q2 = jnp.where(lane >= DIFF_HALF_DIM, qf, 0.0).astype(BF16)
    pq = pq_ref[...]
    qa = (jnp.concatenate([q1, pq], axis=1), jnp.concatenate([q2, pq], axis=1))
    qd = (q1, q2)

    row = lax.broadcasted_iota(jnp.int32, (tq, tq), 0)
    col = lax.broadcasted_iota(jnp.int32, (tq, tq), 1)
    slope = slope_ref[0:1, 0:1]
    dbias = jnp.where((col // CHUNK) <= (row // CHUNK),
                      -slope * jnp.abs(row - col).astype(F32), -jnp.inf)

    def step(j, carry, diagonal):
        ks = pl.multiple_of(j * tq, tq)
        k = k_ref[pl.ds(ks, tq), :]
        v = v_ref[pl.ds(ks, tq), :]
        out = []
        for t in range(2):
            m, l, acc = carry[t]
            if diagonal:
                s = lax.dot_general(qd[t], k, NT_DIMS, preferred_element_type=F32) + dbias
            else:
                ka = jnp.concatenate([k, pk_ref[pl.ds(ks, tq), :]], axis=1)
                s = lax.dot_general(qa[t], ka, NT_DIMS, preferred_element_type=F32)
            out.append(_online_softmax_step(s, m, l, acc, v))
        return tuple(out)

    init = tuple((jnp.full((tq, 1), NEG_BIG, F32), jnp.zeros((tq, 1), F32),
                  jnp.zeros((tq, LANES), F32)) for _ in range(2))
    carry = lax.fori_loop(0, qi, lambda j, c: step(j, c, False), init)
    carry = step(qi, carry, True)

    lp = lam_ref[...]
    lam = (jnp.exp(jnp.sum(lp[0:1] * lp[1:2], axis=-1, keepdims=True))
           - jnp.exp(jnp.sum(lp[2:3] * lp[3:4], axis=-1, keepdims=True)) + lambda_init)
    o = carry[0][2] / carry[0][1] - lam * (carry[1][2] / carry[1][1])
    ms = jnp.mean(o * o, axis=-1, keepdims=True)
    o = o * lax.rsqrt(ms + RMS_EPS) * g_ref[...]
    o_ref[...] = (o * out_scale).astype(BF16)


def _diff(qkvb, posq, posk, slopes, lam_rows, subln_g, tq, lambda_init):
    b, s, _ = qkvb.shape
    kern = functools.partial(_diff_kernel, out_scale=1.0 - lambda_init, lambda_init=lambda_init)
    return pl.pallas_call(
        kern,
        grid=(b, DIFF_HEADS, s // tq),
        in_specs=[
            pl.BlockSpec((None, tq, LANES), lambda bi, h, i: (bi, i, h)),
            pl.BlockSpec((None, s, LANES), lambda bi, h, i: (bi, 0, DIFF_HEADS + h)),
            pl.BlockSpec((None, s, LANES), lambda bi, h, i: (bi, 0, 2 * DIFF_HEADS + h)),
            pl.BlockSpec((None, tq, LANES), lambda bi, h, i: (h, i, 0)),
            pl.BlockSpec((s, LANES), lambda bi, h, i: (0, 0)),
            pl.BlockSpec((None, SUBLANES, LANES), lambda bi, h, i: (h, 0, 0)),
            pl.BlockSpec((4, LANES), lambda bi, h, i: (0, 0)),
            pl.BlockSpec((1, LANES), lambda bi, h, i: (0, 0)),
        ],
        out_specs=pl.BlockSpec((None, tq, LANES), lambda bi, h, i: (bi, i, h)),
        out_shape=jax.ShapeDtypeStruct((b, s, DIFF_WIDTH), BF16),
        compiler_params=pltpu.CompilerParams(
            dimension_semantics=("parallel", "parallel", "arbitrary"), vmem_limit_bytes=VMEM_LIMIT),
        name="diff",
    )(qkvb, qkvb, qkvb, posq, posk, slopes, lam_rows, subln_g)


def _alibi_operands(s):
    pos = jnp.arange(s, dtype=jnp.int32)
    hi = (pos // 64).astype(F32)
    lo = (pos % 64).astype(F32)
    slopes = jnp.asarray([2.0 ** (-8.0 * (i + 1) / DIFF_HEADS) for i in range(DIFF_HEADS)], F32)
    zq = jnp.zeros((DIFF_HEADS, s, LANES), F32)
    sl = slopes[:, None]
    zq = zq.at[:, :, 0].set(-sl * 64.0 * hi[None]).at[:, :, 1].set(-sl * lo[None])
    zq = zq.at[:, :, 2].set(jnp.broadcast_to(sl * 64.0, (DIFF_HEADS, s)))
    zq = zq.at[:, :, 3].set(jnp.broadcast_to(sl, (DIFF_HEADS, s)))
    zk = jnp.zeros((s, LANES), F32)
    zk = zk.at[:, 0].set(1.0).at[:, 1].set(1.0).at[:, 2].set(hi).at[:, 3].set(lo)
    slope_tab = jnp.broadcast_to(slopes[:, None, None], (DIFF_HEADS, SUBLANES, LANES))
    return zq.astype(BF16), zk.astype(BF16), slope_tab


def _cmpx(xs, i, j):
    hi = jnp.maximum(xs[i], xs[j])
    lo = jnp.minimum(xs[i], xs[j])
    xs[i], xs[j] = hi, lo


def _bitonic_merge_desc(xs):
    n = len(xs)
    d = n // 2
    while d >= 1:
        for i in range(n):
            if (i & d) == 0:
                _cmpx(xs, i, i + d)
        d //= 2


def _sort_desc(xs):
    n = len(xs)
    k = 2
    while k <= n:
        d = k // 2
        while d >= 1:
            for i in range(n):
                l = i ^ d
                if l > i:
                    if (i & k) == 0:
                        _cmpx(xs, i, l)
                    else:
                        _cmpx(xs, l, i)
            d //= 2
        k *= 2


def _top16_desc(xs):
    xs = list(xs)
    _sort_desc(xs)
    for shift in (4, 2, 1):
        other = [pltpu.roll(xs[15 - i], shift, 0) for i in range(16)]
        xs = [jnp.maximum(xs[i], other[i]) for i in range(16)]
        _bitonic_merge_desc(xs)
    return xs


def _sublane_sum(x):
    for shift in (4, 2, 1):
        x = x + pltpu.roll(x, shift, 0)
    return x


def _route_kernel(oa_ref, ob_ref, x_ref, wo_ref, g_ref, wq_ref, keys_ref,
                  h_ref, hn_ref, s1_ref, s2_ref, e1_ref, e2_ref, tau_ref, sc_ref):
    tm = x_ref.shape[0]
    mixed = jnp.concatenate([oa_ref[...], ob_ref[...]], axis=1)
    h = x_ref[...] + jnp.dot(mixed, wo_ref[...], preferred_element_type=F32)
    h_ref[...] = h
    ms = jnp.mean(h * h, axis=-1, keepdims=True)
    hn = (h * lax.rsqrt(ms + RMS_EPS) * g_ref[...]).astype(BF16)
    hn_ref[...] = hn
    q = jnp.dot(hn, wq_ref[...], preferred_element_type=F32).astype(BF16)
    for hp in range(2 * PEER_HEADS):
        qs = q[:, hp * PEER_HALF_DIM:(hp + 1) * PEER_HALF_DIM]
        sc_ref[hp] = lax.dot_general(keys_ref[hp], qs, NT_DIMS, preferred_element_type=F32)

    sub = lax.broadcasted_iota(jnp.int32, (SUBLANES, tm), 0)
    neg_inf = jnp.full((SUBLANES, tm), -jnp.inf, F32)

    def spread(vals):
        out = vals[SUBLANES - 1]
        for r in range(SUBLANES - 2, -1, -1):
            out = jnp.where(sub == r, vals[r], out)
        return out

    def head_body(hd, _):
        s1 = sc_ref[2 * hd]
        s2 = sc_ref[2 * hd + 1]
        groups = PEER_N_KEYS // SUBLANES
        a = _top16_desc([s1[g * SUBLANES:(g + 1) * SUBLANES, :] for g in range(groups)])
        b = _top16_desc([s2[g * SUBLANES:(g + 1) * SUBLANES, :] for g in range(groups)])
        b_lo, b_hi, a_hi = spread(b[:8]), spread(b[8:]), spread(a[8:])
        cands = [a[0] + b_lo, a[0] + b_hi, a[1] + b_lo]
        for i, n in ((2, 5), (3, 4), (4, 3), (5, 2), (6, 2), (7, 2)):
            cands.append(jnp.where(sub < n, a[i] + b_lo, neg_inf))
        cands.append(a_hi + b[0])
        top = _top16_desc(cands + [neg_inf] * (16 - len(cands)))
        tau = top[PEER_TOPK - 1]
        mx = top[0]
        z = jnp.zeros((SUBLANES, tm), F32)
        for c in cands:
            z = z + jnp.where(c >= tau, jnp.exp(c - mx), 0.0)
        inv_z = 1.0 / _sublane_sum(z)
        s1_ref[hd] = s1
        s2_ref[hd] = s2
        e1_ref[hd] = jnp.exp(s1 - a[0][0:1, :])
        e2_ref[hd] = jnp.exp(s2 - b[0][0:1, :]) * inv_z[0:1, :]
        tau_ref[hd] = tau
        return 0

    lax.fori_loop(0, PEER_HEADS, head_body, 0)


def _route(oa, ob, x2, wo, g2, wq, keys, tm):
    t, d = x2.shape
    nq = wq.shape[1]
    row = lambda i: (i, 0)
    const2 = lambda i: (0, 0)
    tok3 = lambda i: (0, 0, i)
    score_spec = pl.BlockSpec((PEER_HEADS, PEER_N_KEYS, tm), tok3)
    score_shape = jax.ShapeDtypeStruct((PEER_HEADS, PEER_N_KEYS, t), F32)
    return pl.pallas_call(
        _route_kernel,
        grid=(t // tm,),
        in_specs=[
            pl.BlockSpec((tm, FOX_WIDTH), row),
            pl.BlockSpec((tm, DIFF_WIDTH), row),
            pl.BlockSpec((tm, d), row),
            pl.BlockSpec((d, d), const2),
            pl.BlockSpec((1, d), const2),
            pl.BlockSpec((d, nq), const2),
            pl.BlockSpec((2 * PEER_HEADS, PEER_N_KEYS, PEER_HALF_DIM), lambda i: (0, 0, 0)),
        ],
        out_specs=[
            pl.BlockSpec((tm, d), row),
            pl.BlockSpec((tm, d), row),
            score_spec, score_spec, score_spec, score_spec,
            pl.BlockSpec((PEER_HEADS, SUBLANES, tm), tok3),
        ],
        out_shape=[
            jax.ShapeDtypeStruct((t, d), F32),
            jax.ShapeDtypeStruct((t, d), BF16),
            score_shape, score_shape, score_shape, score_shape,
            jax.ShapeDtypeStruct((PEER_HEADS, SUBLANES, t), F32),
        ],
        scratch_shapes=[pltpu.VMEM((2 * PEER_HEADS, PEER_N_KEYS, tm), F32)],
        compiler_params=pltpu.CompilerParams(
            dimension_semantics=("parallel",), vmem_limit_bytes=VMEM_LIMIT),
        name="route",
    )(oa, ob, x2, wo, g2, wq, keys)


def _gelu(x):
    return 0.5 * x * (1.0 + lax.erf(x * (1.0 / math.sqrt(2.0))))


def _peer_kernel(u_ref, x_ref, vt_ref, s1_ref, s2_ref, e1_ref, e2_ref, tau_ref, h_ref, g_ref,
                 o_ref, acc_ref, ht_ref, wact_ref, *, final_norm):
    j = pl.program_id(1)
    rows = u_ref.shape[0]
    n_a = rows // PEER_N_KEYS
    tt = x_ref.shape[0]

    @pl.when(j == 0)
    def _():
        acc_ref[...] = jnp.zeros_like(acc_ref)

    ht_ref[...] = lax.dot_general(u_ref[...], x_ref[...], NT_DIMS, preferred_element_type=F32)

    sub = lax.broadcasted_iota(jnp.int32, (n_a, LANES), 0)

    def a_body(al, _):
        r0 = pl.multiple_of(al * PEER_N_KEYS, PEER_N_KEYS)
        pick = sub == al
        for lc in range(tt // LANES):
            ls = slice(lc * LANES, (lc + 1) * LANES)
            act = _gelu(ht_ref[pl.ds(r0, PEER_N_KEYS), ls])
            w = jnp.zeros((PEER_N_KEYS, LANES), F32)
            for hd in range(PEER_HEADS):
                s1b = jnp.sum(jnp.where(pick, s1_ref[hd, :, ls], 0.0), axis=0, keepdims=True)
                e1b = jnp.sum(jnp.where(pick, e1_ref[hd, :, ls], 0.0), axis=0, keepdims=True)
                sel = (s1b + s2_ref[hd, :, ls]) >= tau_ref[hd, 0:1, ls]
                w = w + jnp.where(sel, e1b * e2_ref[hd, :, ls], 0.0)
            wact_ref[pl.ds(r0, PEER_N_KEYS), ls] = (act * w).astype(BF16)
        return 0

    lax.fori_loop(0, n_a, a_body, 0)
    acc_ref[...] += jnp.dot(vt_ref[...], wact_ref[...], preferred_element_type=F32)

    @pl.when(j == pl.num_programs(1) - 1)
    def _():
        o = acc_ref[...].T + h_ref[...]
        if final_norm:
            ms = jnp.mean(o * o, axis=-1, keepdims=True)
            o = o * lax.rsqrt(ms + RMS_EPS) * g_ref[...]
        o_ref[...] = o


def _peer(u, hn, vt, s1, s2, e1, e2, tau, h, g, tt, rows, final_norm):
    t, d = hn.shape
    ne = u.shape[0]
    tok3 = lambda i, j: (0, 0, i)
    n_a = rows // PEER_N_KEYS
    assert n_a % SUBLANES == 0
    score_spec = pl.BlockSpec((PEER_HEADS, PEER_N_KEYS, tt), tok3)
    first_spec = pl.BlockSpec((PEER_HEADS, n_a, tt), lambda i, j: (0, j, i))
    return pl.pallas_call(
        functools.partial(_peer_kernel, final_norm=final_norm),
        grid=(t // tt, ne // rows),
        in_specs=[
            pl.BlockSpec((rows, d), lambda i, j: (j, 0)),
            pl.BlockSpec((tt, d), lambda i, j: (i, 0)),
            pl.BlockSpec((d, rows), lambda i, j: (0, j)),
            first_spec, score_spec, first_spec, score_spec,
            pl.BlockSpec((PEER_HEADS, SUBLANES, tt), tok3),
            pl.BlockSpec((tt, d), lambda i, j: (i, 0)),
            pl.BlockSpec((1, d), lambda i, j: (0, 0)),
        ],
        out_specs=pl.BlockSpec((tt, d), lambda i, j: (i, 0)),
        out_shape=jax.ShapeDtypeStruct((t, d), F32),
        scratch_shapes=[
            pltpu.VMEM((d, tt), F32),
            pltpu.VMEM((rows, tt), F32),
            pltpu.VMEM((rows, tt), BF16),
        ],
        compiler_params=pltpu.CompilerParams(
            dimension_semantics=("parallel", "arbitrary"), vmem_limit_bytes=VMEM_LIMIT),
        name="peer",
    )(u, hn, vt, s1, s2, e1, e2, tau, h, g)


def _largest_tile(n, cap):
    t = cap
    while n % t:
        t //= 2
    return t


def kernel(x, norm1_g, w_in, b_f, lambda_q1, lambda_k1, lambda_q2, lambda_k2, subln_g, w_out,
           norm2_g, peer_w_q, peer_sub_keys, peer_u, peer_v, final_g):
    b, s, d = x.shape
    depth = w_in.shape[0]
    t = b * s
    tile_s = _largest_tile(s, 256)
    tile_in = _largest_tile(s, 512)
    tile_route = _largest_tile(t, 256)
    tile_peer = _largest_tile(t, 512)
    rows_peer = 1024

    posq, posk, slope_tab = _alibi_operands(s)
    h = x
    for layer in range(depth):
        w = w_in[layer]
        o = 0
        parts = {}
        for name, width in (("qa", FOX_WIDTH), ("ka", FOX_WIDTH), ("va", FOX_WIDTH), ("f", FOX_HEADS),
                            ("qb", DIFF_WIDTH), ("kb", DIFF_WIDTH), ("vb", DIFF_WIDTH)):
            parts[name] = w[:, o:o + width]
            o += width
        gate = jnp.zeros((d, LANES), F32)
        bf128 = jnp.zeros((1, LANES), F32)
        for g0 in GATE_GROUPS:
            gate = gate.at[:, g0:g0 + FOX_HEADS].set(parts["f"])
            bf128 = bf128.at[0, g0:g0 + FOX_HEADS].set(b_f[layer])
        w_all = jnp.concatenate([parts[k] for k in ("qa", "ka", "va", "qb", "kb", "vb")] + [gate],
                                axis=1).astype(BF16)

        qkva, bq, bk, qkvb = _inproj(h, norm1_g[layer][None, :], w_all, bf128, tile_in)
        out_a = _fox(qkva, bq, bk, tile_s)

        lambda_init = 0.8 - 0.6 * math.exp(-0.3 * layer)
        lam_rows = jnp.zeros((4, LANES), F32)
        for r, p in enumerate((lambda_q1, lambda_k1, lambda_q2, lambda_k2)):
            lam_rows = lam_rows.at[r, :DIFF_HALF_DIM].set(p[layer].astype(F32))
        out_b = _diff(qkvb, posq, posk, slope_tab, lam_rows, subln_g[layer][None, :].astype(F32),
                      tile_s, lambda_init)

        keys = peer_sub_keys[layer].reshape(2 * PEER_HEADS, PEER_N_KEYS, PEER_HALF_DIM).astype(BF16)
        h2, hn, s1, s2, e1, e2, tau = _route(
            out_a.reshape(t, FOX_WIDTH), out_b.reshape(t, DIFF_WIDTH), h.reshape(t, d),
            w_out[layer].astype(BF16), norm2_g[layer][None, :], peer_w_q[layer].astype(BF16), keys,
            tile_route)

        last = layer == depth - 1
        h = _peer(peer_u[layer].astype(BF16), hn, peer_v[layer].T.astype(BF16), s1, s2, e1, e2, tau,
                  h2, final_g[None, :], tile_peer, rows_peer, last).reshape(b, s, d)
    return h
```

```python
import functools
import math

import jax
import jax.numpy as jnp
from jax import lax
from jax.experimental import pallas as pl
from jax.experimental.pallas import tpu as pltpu

F32 = jnp.float32
BF16 = jnp.bfloat16

D_MODEL = 1024
FOX_HEADS = 8
FOX_HEAD_DIM = 64
FOX_WIDTH = FOX_HEADS * FOX_HEAD_DIM
DIFF_HEADS = 4
DIFF_HALF_DIM = 64
DIFF_HEAD_DIM = 2 * DIFF_HALF_DIM
DIFF_WIDTH = DIFF_HEADS * DIFF_HEAD_DIM
CHUNK = 64
PEER_HEADS = 8
PEER_N_KEYS = 128
PEER_N_EXPERTS = PEER_N_KEYS * PEER_N_KEYS
PEER_HALF_DIM = 128
PEER_TOPK = 16
RMS_EPS = 1e-6

LANES = 128
SUBLANES = 8
NEG_BIG = -1e30
VMEM_LIMIT = 56 * 1024 * 1024

GATE_GROUPS = (0, 8, 16, 32, 40, 48)

NT_DIMS = (((1,), (1,)), ((), ()))


def _split3(v):
    hi = v.astype(BF16)
    r1 = v - hi.astype(F32)
    mid = r1.astype(BF16)
    lo = (r1 - mid.astype(F32)).astype(BF16)
    return hi, mid, lo


def _inproj_kernel(x_ref, g_ref, w_ref, wkt_ref, bf_ref, tril_ref,
                   qa_ref, bq_ref, kta_ref, bkt_ref, va_ref, qb_ref, ktb_ref, vb_ref, carry_ref):
    tm = x_ref.shape[0]

    @pl.when(pl.program_id(1) == 0)
    def _():
        carry_ref[...] = jnp.zeros_like(carry_ref)

    x = x_ref[...]
    ms = jnp.mean(x * x, axis=-1, keepdims=True)
    xb = (x * lax.rsqrt(ms + RMS_EPS) * g_ref[...]).astype(BF16)
    proj = jnp.dot(xb, w_ref[...], preferred_element_type=F32)
    kt = lax.dot_general(wkt_ref[...], xb, NT_DIMS, preferred_element_type=F32)

    o_va = FOX_WIDTH
    o_qb = o_va + 2 * FOX_WIDTH
    o_vb = o_qb + DIFF_WIDTH
    o_gate = o_vb + DIFF_WIDTH
    qa_ref[...] = (proj[:, 0:FOX_WIDTH] * 0.125).astype(BF16)
    qb_ref[...] = (proj[:, o_qb:o_qb + DIFF_WIDTH] * 0.125).astype(BF16)
    vb_ref[...] = proj[:, o_vb:o_vb + DIFF_WIDTH].astype(BF16)
    kta_ref[...] = kt[0:FOX_WIDTH, :].astype(BF16)
    ktb_ref[...] = kt[FOX_WIDTH:FOX_WIDTH + DIFF_WIDTH, :].astype(BF16)
    va = proj[:, o_va:o_qb]
    vlane = lax.broadcasted_iota(jnp.int32, va.shape, 1)
    va_ref[...] = jnp.where((vlane & FOX_HEAD_DIM) != 0, 1.0, va).astype(BF16)

    z = proj[:, o_gate:o_gate + LANES] + bf_ref[...]
    logf = jnp.minimum(z, 0.0) - jnp.log1p(jnp.exp(-jnp.abs(z)))
    tril = tril_ref[...]
    hi, mid, lo = _split3(logf)
    c = (jnp.dot(tril, hi, preferred_element_type=F32)
         + jnp.dot(tril, mid, preferred_element_type=F32)
         + jnp.dot(tril, lo, preferred_element_type=F32)) + carry_ref[0:1, :]
    carry_ref[...] = jnp.broadcast_to(c[tm - 1:tm, :], carry_ref.shape)

    chi, cmid, clo = _split3(c)
    chi, cmid, clo = chi.astype(F32), cmid.astype(F32), clo.astype(F32)
    lane = lax.broadcasted_iota(jnp.int32, c.shape, 1)
    ones_hi = (lane >= 32) & (lane < 56)
    bq = jnp.where(lane < 8, chi, jnp.where(lane < 16, cmid, jnp.where(lane < 24, clo,
                   jnp.where(ones_hi, 1.0, 0.0))))
    bk = jnp.where(lane < 24, 1.0, jnp.where(lane < 32, 0.0, jnp.where(lane < 40, -chi,
                   jnp.where(lane < 48, -cmid, jnp.where(lane < 56, -clo, 0.0)))))
    bq_ref[...] = bq.astype(BF16)
    bkt_ref[...] = bk.T.astype(BF16)


def _inproj(x, g, w_all, w_kt, bf128, tm):
    b, s, d = x.shape
    ncol = w_all.shape[1]
    nkt = w_kt.shape[0]
    tril = jnp.tril(jnp.ones((tm, tm), F32)).astype(BF16)
    row = lambda bi, si: (bi, si, 0)
    col = lambda bi, si: (bi, 0, si)
    const = lambda bi, si: (0, 0)
    shapes = [
        ((b, s, FOX_WIDTH), pl.BlockSpec((None, tm, FOX_WIDTH), row)),
        ((b, s, LANES), pl.BlockSpec((None, tm, LANES), row)),
        ((b, FOX_WIDTH, s), pl.BlockSpec((None, FOX_WIDTH, tm), col)),
        ((b, LANES, s), pl.BlockSpec((None, LANES, tm), col)),
        ((b, s, 2 * FOX_WIDTH), pl.BlockSpec((None, tm, 2 * FOX_WIDTH), row)),
        ((b, s, DIFF_WIDTH), pl.BlockSpec((None, tm, DIFF_WIDTH), row)),
        ((b, DIFF_WIDTH, s), pl.BlockSpec((None, DIFF_WIDTH, tm), col)),
        ((b, s, DIFF_WIDTH), pl.BlockSpec((None, tm, DIFF_WIDTH), row)),
    ]
    return pl.pallas_call(
        _inproj_kernel,
        grid=(b, s // tm),
        in_specs=[
            pl.BlockSpec((None, tm, d), row),
            pl.BlockSpec((1, d), const),
            pl.BlockSpec((d, ncol), const),
            pl.BlockSpec((nkt, d), const),
            pl.BlockSpec((1, LANES), const),
            pl.BlockSpec((tm, tm), const),
        ],
        out_specs=[spec for _, spec in shapes],
        out_shape=[jax.ShapeDtypeStruct(shape, BF16) for shape, _ in shapes],
        scratch_shapes=[pltpu.VMEM((SUBLANES, LANES), F32)],
        compiler_params=pltpu.CompilerParams(
            dimension_semantics=("arbitrary", "arbitrary"), vmem_limit_bytes=VMEM_LIMIT),
        name="inproj",
    )(x, g, w_all, w_kt, bf128, tril)


def _softmax_update(s, m, acc, v_aug):
    m_new = jnp.maximum(m, jnp.max(s, axis=-1, keepdims=True))
    p = jnp.exp(s - m_new).astype(BF16)
    acc_new = jnp.exp(m - m_new) * acc + jnp.dot(p, v_aug, preferred_element_type=F32)
    return m_new, acc_new


def _fox_kernel(q_ref, bq_ref, kt_ref, bkt_ref, v0_ref, v1_ref, o_ref):
    tq = q_ref.shape[0]
    pair = pl.program_id(1)
    qi = pl.program_id(2)
    lane = lax.broadcasted_iota(jnp.int32, (tq, LANES), 1)
    qf = q_ref[...].astype(F32)
    bqf = bq_ref[...].astype(F32)
    qa = []
    for hh in range(2):
        head = 2 * pair + hh
        qm = jnp.where((lane >= FOX_HEAD_DIM * hh) & (lane < FOX_HEAD_DIM * (hh + 1)), qf, 0.0)
        bm = jnp.where(((lane & 7) == head) & (lane < 56), bqf, 0.0)
        qa.append(jnp.concatenate([qm, bm], axis=1).astype(BF16))
    q_all = jnp.concatenate(qa, axis=0)
    v_refs = (v0_ref, v1_ref)

    def step(j, carry, causal):
        ks = pl.multiple_of(j * tq, tq)
        kt = jnp.concatenate([kt_ref[:, pl.ds(ks, tq)], bkt_ref[:, pl.ds(ks, tq)]], axis=0)
        s_all = jnp.dot(q_all, kt, preferred_element_type=F32)
        out = []
        for hh in range(2):
            s = s_all[hh * tq:(hh + 1) * tq]
            if causal:
                row = lax.broadcasted_iota(jnp.int32, s.shape, 0)
                col = lax.broadcasted_iota(jnp.int32, s.shape, 1)
                s = jnp.where(col <= row, s, -jnp.inf)
            m, acc = carry[hh]
            out.append(_softmax_update(s, m, acc, v_refs[hh][pl.ds(ks, tq), :]))
        return tuple(out)

    init = tuple((jnp.full((tq, 1), NEG_BIG, F32), jnp.zeros((tq, LANES), F32)) for _ in range(2))
    carry = lax.fori_loop(0, qi, lambda j, c: step(j, c, False), init)
    (_, acc0), (_, acc1) = step(qi, carry, True)
    o0 = acc0 / pltpu.roll(acc0, FOX_HEAD_DIM, 1)
    o1 = pltpu.roll(acc1, FOX_HEAD_DIM, 1) / acc1
    o_ref[...] = jnp.where(lane < FOX_HEAD_DIM, o0, o1).astype(BF16)


def _fox(qa, bq, kta, bkt, vaa, tq):
    b, s, _ = qa.shape
    npair = FOX_HEADS // 2
    return pl.pallas_call(
        _fox_kernel,
        grid=(b, npair, s // tq),
        in_specs=[
            pl.BlockSpec((None, tq, LANES), lambda bi, p, i: (bi, i, p)),
            pl.BlockSpec((None, tq, LANES), lambda bi, p, i: (bi, i, 0)),
            pl.BlockSpec((None, LANES, s), lambda bi, p, i: (bi, p, 0)),
            pl.BlockSpec((None, LANES, s), lambda bi, p, i: (bi, 0, 0)),
            pl.BlockSpec((None, s, LANES), lambda bi, p, i: (bi, 0, 2 * p)),
            pl.BlockSpec((None, s, LANES), lambda bi, p, i: (bi, 0, 2 * p + 1)),
        ],
        out_specs=pl.BlockSpec((None, tq, LANES), lambda bi, p, i: (bi, i, p)),
        out_shape=jax.ShapeDtypeStruct((b, s, FOX_WIDTH), BF16),
        compiler_params=pltpu.CompilerParams(
            dimension_semantics=("parallel", "parallel", "arbitrary"), vmem_limit_bytes=VMEM_LIMIT),
        name="fox",
    )(qa, bq, kta, bkt, vaa, vaa)


def _diff_kernel(q_ref, pq_ref, kt_ref, pkt_ref, v_ref, slope_ref, lam_ref, g_ref, o_ref, *, out_scale,
                 lambda_init):
    tq = q_ref.shape[0]
    qi = pl.program_id(2)
    lane = lax.broadcasted_iota(jnp.int32, (tq, LANES), 1)
    qf = q_ref[...].astype(F32)
    pq = pq_ref[...]
    q1 = jnp.where(lane < DIFF_HALF_DIM, qf, 0.0).astype(BF16)
    q2 = jnp.where(lane >= DIFF_HALF_DIM, qf, 0.0).astype(BF16)
    q_all = jnp.concatenate([jnp.concatenate([q1, pq], axis=1),
                             jnp.concatenate([q2, pq], axis=1)], axis=0)
    ones = jnp.ones((tq, LANES), BF16)

    def step(j, carry, diagonal):
        ks = pl.multiple_of(j * tq, tq)
        kt = jnp.concatenate([kt_ref[:, pl.ds(ks, tq)], pkt_ref[:, pl.ds(ks, tq)]], axis=0)
        s_all = jnp.dot(q_all, kt, preferred_element_type=F32)
        v_aug = jnp.concatenate([v_ref[pl.ds(ks, tq), :], ones], axis=1)
        if diagonal:
            row = lax.broadcasted_iota(jnp.int32, (tq, tq), 0)
            col = lax.broadcasted_iota(jnp.int32, (tq, tq), 1)
            fix = jnp.where((col // CHUNK) <= (row // CHUNK),
                            (-2.0 * slope_ref[0:1, 0:1]) * jnp.maximum(col - row, 0).astype(F32), -jnp.inf)
        out = []
        for t in range(2):
            s = s_all[t * tq:(t + 1) * tq]
            if diagonal:
                s = s + fix
            m, acc = carry[t]
            out.append(_softmax_update(s, m, acc, v_aug))
        return tuple(out)

    init = tuple((jnp.full((tq, 1), NEG_BIG, F32), jnp.zeros((tq, 2 * LANES), F32)) for _ in range(2))
    carry = lax.fori_loop(0, qi, lambda j, c: step(j, c, False), init)
    (_, acc1), (_, acc2) = step(qi, carry, True)

    lp = lam_ref[...]
    lam = (jnp.exp(jnp.sum(lp[0:1] * lp[1:2], axis=-1, keepdims=True))
           - jnp.exp(jnp.sum(lp[2:3] * lp[3:4], axis=-1, keepdims=True)) + lambda_init)
    o = acc1[:, :LANES] / acc1[:, LANES:] - lam * (acc2[:, :LANES] / acc2[:, LANES:])
    ms = jnp.mean(o * o, axis=-1, keepdims=True)
    o = o * lax.rsqrt(ms + RMS_EPS) * g_ref[...]
    o_ref[...] = (o * out_scale).astype(BF16)


def _diff(qb, posq, ktb, poskt, vb, slopes, lam_rows, subln_g, tq, lambda_init):
    b, s, _ = qb.shape
    kern = functools.partial(_diff_kernel, out_scale=1.0 - lambda_init, lambda_init=lambda_init)
    return pl.pallas_call(
        kern,
        grid=(b, DIFF_HEADS, s // tq),
        in_specs=[
            pl.BlockSpec((None, tq, LANES), lambda bi, h, i: (bi, i, h)),
            pl.BlockSpec((None, tq, LANES), lambda bi, h, i: (h, i, 0)),
            pl.BlockSpec((None, LANES, s), lambda bi, h, i: (bi, h, 0)),
            pl.BlockSpec((LANES, s), lambda bi, h, i: (0, 0)),
            pl.BlockSpec((None, s, LANES), lambda bi, h, i: (bi, 0, h)),
            pl.BlockSpec((None, SUBLANES, LANES), lambda bi, h, i: (h, 0, 0)),
            pl.BlockSpec((4, LANES), lambda bi, h, i: (0, 0)),
            pl.BlockSpec((1, LANES), lambda bi, h, i: (0, 0)),
        ],
        out_specs=pl.BlockSpec((None, tq, LANES), lambda bi, h, i: (bi, i, h)),
        out_shape=jax.ShapeDtypeStruct((b, s, DIFF_WIDTH), BF16),
        compiler_params=pltpu.CompilerParams(
            dimension_semantics=("parallel", "parallel", "arbitrary"), vmem_limit_bytes=VMEM_LIMIT),
        name="diff",
    )(qb, posq, ktb, poskt, vb, slopes, lam_rows, subln_g)


def _alibi_operands(s):
    pos = jnp.arange(s, dtype=jnp.int32)
    hi = (pos // 64).astype(F32)
    lo = (pos % 64).astype(F32)
    slopes = jnp.asarray([2.0 ** (-8.0 * (i + 1) / DIFF_HEADS) for i in range(DIFF_HEADS)], F32)
    zq = jnp.zeros((DIFF_HEADS, s, LANES), F32)
    sl = slopes[:, None]
    zq = zq.at[:, :, 0].set(-sl * 64.0 * hi[None]).at[:, :, 1].set(-sl * lo[None])
    zq = zq.at[:, :, 2].set(jnp.broadcast_to(sl * 64.0, (DIFF_HEADS, s)))
    zq = zq.at[:, :, 3].set(jnp.broadcast_to(sl, (DIFF_HEADS, s)))
    zk = jnp.zeros((LANES, s), F32)
    zk = zk.at[0].set(1.0).at[1].set(1.0).at[2].set(hi).at[3].set(lo)
    slope_tab = jnp.broadcast_to(slopes[:, None, None], (DIFF_HEADS, SUBLANES, LANES))
    return zq.astype(BF16), zk.astype(BF16), slope_tab


def _cmpx(xs, i, j):
    hi = jnp.maximum(xs[i], xs[j])
    lo = jnp.minimum(xs[i], xs[j])
    xs[i], xs[j] = hi, lo


def _bitonic_merge_desc(xs):
    n = len(xs)
    d = n // 2
    while d >= 1:
        for i in range(n):
            if (i & d) == 0:
                _cmpx(xs, i, i + d)
        d //= 2


def _sort_desc(xs):
    n = len(xs)
    k = 2
    while k <= n:
        d = k // 2
        while d >= 1:
            for i in range(n):
                l = i ^ d
                if l > i:
                    if (i & k) == 0:
                        _cmpx(xs, i, l)
                    else:
                        _cmpx(xs, l, i)
            d //= 2
        k *= 2


def _top16_desc(xs):
    xs = list(xs)
    _sort_desc(xs)
    for shift in (4, 2, 1):
        other = [pltpu.roll(xs[15 - i], shift, 0) for i in range(16)]
        xs = [jnp.maximum(xs[i], other[i]) for i in range(16)]
        _bitonic_merge_desc(xs)
    return xs


def _sublane_sum(x):
    for shift in (4, 2, 1):
        x = x + pltpu.roll(x, shift, 0)
    return x


def _route_kernel(oa_ref, ob_ref, x_ref, wo_ref, g_ref, wq_ref, keys_ref,
                  h_ref, hn_ref, s1_ref, s2_ref, e1_ref, e2_ref, tau_ref, sc_ref):
    tm = x_ref.shape[0]
    mixed = jnp.concatenate([oa_ref[...], ob_ref[...]], axis=1)
    h = x_ref[...] + jnp.dot(mixed, wo_ref[...], preferred_element_type=F32)
    h_ref[...] = h
    ms = jnp.mean(h * h, axis=-1, keepdims=True)
    hn = (h * lax.rsqrt(ms + RMS_EPS) * g_ref[...]).astype(BF16)
    hn_ref[...] = hn
    q = jnp.dot(hn, wq_ref[...], preferred_element_type=F32).astype(BF16)
    for hp in range(2 * PEER_HEADS):
        qs = q[:, hp * PEER_HALF_DIM:(hp + 1) * PEER_HALF_DIM]
        sc_ref[hp] = lax.dot_general(keys_ref[hp], qs, NT_DIMS, preferred_element_type=F32)

    sub = lax.broadcasted_iota(jnp.int32, (SUBLANES, tm), 0)
    neg_inf = jnp.full((SUBLANES, tm), -jnp.inf, F32)

    def spread(vals):
        out = vals[SUBLANES - 1]
        for r in range(SUBLANES - 2, -1, -1):
            out = jnp.where(sub == r, vals[r], out)
        return out

    def head_body(hd, _):
        s1 = sc_ref[2 * hd]
        s2 = sc_ref[2 * hd + 1]
        groups = PEER_N_KEYS // SUBLANES
        a = _top16_desc([s1[g * SUBLANES:(g + 1) * SUBLANES, :] for g in range(groups)])
        b = _top16_desc([s2[g * SUBLANES:(g + 1) * SUBLANES, :] for g in range(groups)])
        b_lo, b_hi, a_hi = spread(b[:8]), spread(b[8:]), spread(a[8:])
        cands = [a[0] + b_lo, a[0] + b_hi, a[1] + b_lo]
        for i, n in ((2, 5), (3, 4), (4, 3), (5, 2), (6, 2), (7, 2)):
            cands.append(jnp.where(sub < n, a[i] + b_lo, neg_inf))
        cands.append(a_hi + b[0])
        top = _top16_desc(cands + [neg_inf] * (16 - len(cands)))
        tau = top[PEER_TOPK - 1]
        mx = top[0]
        z = jnp.zeros((SUBLANES, tm), F32)
        for c in cands:
            z = z + jnp.where(c >= tau, jnp.exp(c - mx), 0.0)
        inv_z = 1.0 / _sublane_sum(z)
        s1_ref[hd] = s1
        s2_ref[hd] = s2
        e1_ref[hd] = jnp.exp(s1 - a[0][0:1, :])
        e2_ref[hd] = jnp.exp(s2 - b[0][0:1, :]) * inv_z[0:1, :]
        tau_ref[hd] = tau
        return 0

    lax.fori_loop(0, PEER_HEADS, head_body, 0)


def _route(oa, ob, x2, wo, g2, wq, keys, tm):
    t, d = x2.shape
    nq = wq.shape[1]
    row = lambda i: (i, 0)
    const2 = lambda i: (0, 0)
    tok3 = lambda i: (0, 0, i)
    score_spec = pl.BlockSpec((PEER_HEADS, PEER_N_KEYS, tm), tok3)
    score_shape = jax.ShapeDtypeStruct((PEER_HEADS, PEER_N_KEYS, t), F32)
    return pl.pallas_call(
        _route_kernel,
        grid=(t // tm,),
        in_specs=[
            pl.BlockSpec((tm, FOX_WIDTH), row),
            pl.BlockSpec((tm, DIFF_WIDTH), row),
            pl.BlockSpec((tm, d), row),
            pl.BlockSpec((d, d), const2),
            pl.BlockSpec((1, d), const2),
            pl.BlockSpec((d, nq), const2),
            pl.BlockSpec((2 * PEER_HEADS, PEER_N_KEYS, PEER_HALF_DIM), lambda i: (0, 0, 0)),
        ],
        out_specs=[
            pl.BlockSpec((tm, d), row),
            pl.BlockSpec((tm, d), row),
            score_spec, score_spec, score_spec, score_spec,
            pl.BlockSpec((PEER_HEADS, SUBLANES, tm), tok3),
        ],
        out_shape=[
            jax.ShapeDtypeStruct((t, d), F32),
            jax.ShapeDtypeStruct((t, d), BF16),
            score_shape, score_shape, score_shape, score_shape,
            jax.ShapeDtypeStruct((PEER_HEADS, SUBLANES, t), F32),
        ],
        scratch_shapes=[pltpu.VMEM((2 * PEER_HEADS, PEER_N_KEYS, tm), F32)],
        compiler_params=pltpu.CompilerParams(
            dimension_semantics=("parallel",), vmem_limit_bytes=VMEM_LIMIT),
        name="route",
    )(oa, ob, x2, wo, g2, wq, keys)


def _gelu(x):
    return 0.5 * x * (1.0 + lax.erf(x * (1.0 / math.sqrt(2.0))))


def _peer_kernel(u_ref, x_ref, vt_ref, s1_ref, s2_ref, e1_ref, e2_ref, tau_ref, h_ref, g_ref,
                 o_ref, acc_ref, ht_ref, wact_ref, *, final_norm):
    j = pl.program_id(1)
    rows = u_ref.shape[0]
    n_a = rows // PEER_N_KEYS
    tt = x_ref.shape[0]

    @pl.when(j == 0)
    def _():
        acc_ref[...] = jnp.zeros_like(acc_ref)

    ht_ref[...] = lax.dot_general(u_ref[...], x_ref[...], NT_DIMS, preferred_element_type=F32)

    sub = lax.broadcasted_iota(jnp.int32, (n_a, LANES), 0)

    def a_body(al, _):
        r0 = pl.multiple_of(al * PEER_N_KEYS, PEER_N_KEYS)
        pick = sub == al
        for lc in range(tt // LANES):
            ls = slice(lc * LANES, (lc + 1) * LANES)
            act = _gelu(ht_ref[pl.ds(r0, PEER_N_KEYS), ls])
            w = jnp.zeros((PEER_N_KEYS, LANES), F32)
            for hd in range(PEER_HEADS):
                s1b = jnp.sum(jnp.where(pick, s1_ref[hd, :, ls], 0.0), axis=0, keepdims=True)
                e1b = jnp.sum(jnp.where(pick, e1_ref[hd, :, ls], 0.0), axis=0, keepdims=True)
                sel = (s1b + s2_ref[hd, :, ls]) >= tau_ref[hd, 0:1, ls]
                w = w + jnp.where(sel, e1b * e2_ref[hd, :, ls], 0.0)
            wact_ref[pl.ds(r0, PEER_N_KEYS), ls] = (act * w).astype(BF16)
        return 0

    lax.fori_loop(0, n_a, a_body, 0)
    acc_ref[...] += jnp.dot(vt_ref[...], wact_ref[...], preferred_element_type=F32)

    @pl.when(j == pl.num_programs(1) - 1)
    def _():
        o = acc_ref[...].T + h_ref[...]
        if final_norm:
            ms = jnp.mean(o * o, axis=-1, keepdims=True)
            o = o * lax.rsqrt(ms + RMS_EPS) * g_ref[...]
        o_ref[...] = o


def _peer(u, hn, vt, s1, s2, e1, e2, tau, h, g, tt, rows, final_norm):
    t, d = hn.shape
    ne = u.shape[0]
    tok3 = lambda i, j: (0, 0, i)
    n_a = rows // PEER_N_KEYS
    assert n_a % SUBLANES == 0
    score_spec = pl.BlockSpec((PEER_HEADS, PEER_N_KEYS, tt), tok3)
    first_spec = pl.BlockSpec((PEER_HEADS, n_a, tt), lambda i, j: (0, j, i))
    return pl.pallas_call(
        functools.partial(_peer_kernel, final_norm=final_norm),
        grid=(t // tt, ne // rows),
        in_specs=[
            pl.BlockSpec((rows, d), lambda i, j: (j, 0)),
            pl.BlockSpec((tt, d), lambda i, j: (i, 0)),
            pl.BlockSpec((d, rows), lambda i, j: (0, j)),
            first_spec, score_spec, first_spec, score_spec,
            pl.BlockSpec((PEER_HEADS, SUBLANES, tt), tok3),
            pl.BlockSpec((tt, d), lambda i, j: (i, 0)),
            pl.BlockSpec((1, d), lambda i, j: (0, 0)),
        ],
        out_specs=pl.BlockSpec((tt, d), lambda i, j: (i, 0)),
        out_shape=jax.ShapeDtypeStruct((t, d), F32),
        scratch_shapes=[
            pltpu.VMEM((d, tt), F32),
            pltpu.VMEM((rows, tt), F32),
            pltpu.VMEM((rows, tt), BF16),
        ],
        compiler_params=pltpu.CompilerParams(
            dimension_semantics=("parallel", "arbitrary"), vmem_limit_bytes=VMEM_LIMIT),
        name="peer",
    )(u, hn, vt, s1, s2, e1, e2, tau, h, g)


def _largest_tile(n, cap):
    t = cap
    while n % t:
        t //= 2
    return t


def kernel(x, norm1_g, w_in, b_f, lambda_q1, lambda_k1, lambda_q2, lambda_k2, subln_g, w_out,
           norm2_g, peer_w_q, peer_sub_keys, peer_u, peer_v, final_g):
    b, s, d = x.shape
    depth = w_in.shape[0]
    t = b * s
    tile_s = _largest_tile(s, 512)
    tile_in = _largest_tile(s, 512)
    tile_route = _largest_tile(t, 256)
    tile_peer = _largest_tile(t, 512)
    rows_peer = 1024

    posq, poskt, slope_tab = _alibi_operands(s)
    h = x
    for layer in range(depth):
        w = w_in[layer]
        o = 0
        parts = {}
        for name, width in (("qa", FOX_WIDTH), ("ka", FOX_WIDTH), ("va", FOX_WIDTH), ("f", FOX_HEADS),
                            ("qb", DIFF_WIDTH), ("kb", DIFF_WIDTH), ("vb", DIFF_WIDTH)):
            parts[name] = w[:, o:o + width]
            o += width
        gate = jnp.zeros((d, LANES), F32)
        bf128 = jnp.zeros((1, LANES), F32)
        for g0 in GATE_GROUPS:
            gate = gate.at[:, g0:g0 + FOX_HEADS].set(parts["f"])
            bf128 = bf128.at[0, g0:g0 + FOX_HEADS].set(b_f[layer])
        va_pad = jnp.pad(parts["va"].reshape(d, FOX_HEADS, FOX_HEAD_DIM),
                         ((0, 0), (0, 0), (0, FOX_HEAD_DIM))).reshape(d, 2 * FOX_WIDTH)
        w_all = jnp.concatenate([parts["qa"], va_pad, parts["qb"], parts["vb"], gate], axis=1).astype(BF16)
        w_kt = jnp.concatenate([parts["ka"], parts["kb"]], axis=1).T.astype(BF16)

        qa, bq, kta, bkt, vaa, qb, ktb, vb = _inproj(h, norm1_g[layer][None, :], w_all, w_kt, bf128, tile_in)
        out_a = _fox(qa, bq, kta, bkt, vaa, tile_s)

        lambda_init = 0.8 - 0.6 * math.exp(-0.3 * layer)
        lam_rows = jnp.zeros((4, LANES), F32)
        for r, p in enumerate((lambda_q1, lambda_k1, lambda_q2, lambda_k2)):
            lam_rows = lam_rows.at[r, :DIFF_HALF_DIM].set(p[layer].astype(F32))
        out_b = _diff(qb, posq, ktb, poskt, vb, slope_tab, lam_rows, subln_g[layer][None, :].astype(F32),
                      tile_s, lambda_init)

        keys = peer_sub_keys[layer].reshape(2 * PEER_HEADS, PEER_N_KEYS, PEER_HALF_DIM).astype(BF16)
        h2, hn, s1, s2, e1, e2, tau = _route(
            out_a.reshape(t, FOX_WIDTH), out_b.reshape(t, DIFF_WIDTH), h.reshape(t, d),
            w_out[layer].astype(BF16), norm2_g[layer][None, :], peer_w_q[layer].astype(BF16), keys,
            tile_route)

        last = layer == depth - 1
        h = _peer(peer_u[layer].astype(BF16), hn, peer_v[layer].T.astype(BF16), s1, s2, e1, e2, tau,
                  h2, final_g[None, :], tile_peer, rows_peer, last).reshape(b, s, d)
    return h
```

```python
import functools
import math

import jax
import jax.numpy as jnp
from jax import lax
from jax.experimental import pallas as pl
from jax.experimental.pallas import tpu as pltpu

F32 = jnp.float32
BF16 = jnp.bfloat16

D_MODEL = 1024
FOX_HEADS = 8
FOX_HEAD_DIM = 64
FOX_WIDTH = FOX_HEADS * FOX_HEAD_DIM
DIFF_HEADS = 4
DIFF_HALF_DIM = 64
DIFF_HEAD_DIM = 2 * DIFF_HALF_DIM
DIFF_WIDTH = DIFF_HEADS * DIFF_HEAD_DIM
CHUNK = 64
PEER_HEADS = 8
PEER_N_KEYS = 128
PEER_N_EXPERTS = PEER_N_KEYS * PEER_N_KEYS
PEER_HALF_DIM = 128
PEER_TOPK = 16
RMS_EPS = 1e-6

LANES = 128
SUBLANES = 8
NEG_BIG = -1e30
VMEM_LIMIT = 56 * 1024 * 1024

GATE_GROUPS = (0, 8, 16, 32, 40, 48)

NT_DIMS = (((1,), (1,)), ((), ()))


def _split3(v):
    hi = v.astype(BF16)
    r1 = v - hi.astype(F32)
    mid = r1.astype(BF16)
    lo = (r1 - mid.astype(F32)).astype(BF16)
    return hi, mid, lo


def _inproj_kernel(x_ref, g_ref, w_ref, wkt_ref, bf_ref, tril_ref,
                   qa_ref, bq_ref, kta_ref, bkt_ref, va_ref, qb_ref, ktb_ref, vb_ref, carry_ref):
    tm = x_ref.shape[0]

    @pl.when(pl.program_id(1) == 0)
    def _():
        carry_ref[...] = jnp.zeros_like(carry_ref)

    x = x_ref[...]
    ms = jnp.mean(x * x, axis=-1, keepdims=True)
    xb = (x * lax.rsqrt(ms + RMS_EPS) * g_ref[...]).astype(BF16)
    proj = jnp.dot(xb, w_ref[...], preferred_element_type=F32)
    kt = lax.dot_general(wkt_ref[...], xb, NT_DIMS, preferred_element_type=F32)

    o_va = FOX_WIDTH
    o_qb = o_va + 2 * FOX_WIDTH
    o_vb = o_qb + DIFF_WIDTH
    o_gate = o_vb + DIFF_WIDTH
    qa_ref[...] = (proj[:, 0:FOX_WIDTH] * 0.125).astype(BF16)
    qb_ref[...] = (proj[:, o_qb:o_qb + DIFF_WIDTH] * 0.125).astype(BF16)
    vb_ref[...] = proj[:, o_vb:o_vb + DIFF_WIDTH].astype(BF16)
    kta_ref[...] = kt[0:FOX_WIDTH, :].astype(BF16)
    ktb_ref[...] = kt[FOX_WIDTH:FOX_WIDTH + DIFF_WIDTH, :].astype(BF16)
    va = proj[:, o_va:o_qb]
    vlane = lax.broadcasted_iota(jnp.int32, va.shape, 1)
    va_ref[...] = jnp.where((vlane & FOX_HEAD_DIM) != 0, 1.0, va).astype(BF16)

    z = proj[:, o_gate:o_gate + LANES] + bf_ref[...]
    logf = jnp.minimum(z, 0.0) - jnp.log1p(jnp.exp(-jnp.abs(z)))
    tril = tril_ref[...]
    hi, mid, lo = _split3(logf)
    c = (jnp.dot(tril, hi, preferred_element_type=F32)
         + jnp.dot(tril, mid, preferred_element_type=F32)
         + jnp.dot(tril, lo, preferred_element_type=F32)) + carry_ref[0:1, :]
    carry_ref[...] = jnp.broadcast_to(c[tm - 1:tm, :], carry_ref.shape)

    chi, cmid, clo = _split3(c)
    chi, cmid, clo = chi.astype(F32), cmid.astype(F32), clo.astype(F32)
    lane = lax.broadcasted_iota(jnp.int32, c.shape, 1)
    ones_hi = (lane >= 32) & (lane < 56)
    bq = jnp.where(lane < 8, chi, jnp.where(lane < 16, cmid, jnp.where(lane < 24, clo,
                   jnp.where(ones_hi, 1.0, 0.0))))
    bk = jnp.where(lane < 24, 1.0, jnp.where(lane < 32, 0.0, jnp.where(lane < 40, -chi,
                   jnp.where(lane < 48, -cmid, jnp.where(lane < 56, -clo, 0.0)))))
    bq_ref[...] = bq.astype(BF16)
    bkt_ref[...] = bk.T.astype(BF16)


def _inproj(x, g, w_all, w_kt, bf128, tm):
    b, s, d = x.shape
    ncol = w_all.shape[1]
    nkt = w_kt.shape[0]
    tril = jnp.tril(jnp.ones((tm, tm), F32)).astype(BF16)
    row = lambda bi, si: (bi, si, 0)
    col = lambda bi, si: (bi, 0, si)
    const = lambda bi, si: (0, 0)
    shapes = [
        ((b, s, FOX_WIDTH), pl.BlockSpec((None, tm, FOX_WIDTH), row)),
        ((b, s, LANES), pl.BlockSpec((None, tm, LANES), row)),
        ((b, FOX_WIDTH, s), pl.BlockSpec((None, FOX_WIDTH, tm), col)),
        ((b, LANES, s), pl.BlockSpec((None, LANES, tm), col)),
        ((b, s, 2 * FOX_WIDTH), pl.BlockSpec((None, tm, 2 * FOX_WIDTH), row)),
        ((b, s, DIFF_WIDTH), pl.BlockSpec((None, tm, DIFF_WIDTH), row)),
        ((b, DIFF_WIDTH, s), pl.BlockSpec((None, DIFF_WIDTH, tm), col)),
        ((b, s, DIFF_WIDTH), pl.BlockSpec((None, tm, DIFF_WIDTH), row)),
    ]
    return pl.pallas_call(
        _inproj_kernel,
        grid=(b, s // tm),
        in_specs=[
            pl.BlockSpec((None, tm, d), row),
            pl.BlockSpec((1, d), const),
            pl.BlockSpec((d, ncol), const),
            pl.BlockSpec((nkt, d), const),
            pl.BlockSpec((1, LANES), const),
            pl.BlockSpec((tm, tm), const),
        ],
        out_specs=[spec for _, spec in shapes],
        out_shape=[jax.ShapeDtypeStruct(shape, BF16) for shape, _ in shapes],
        scratch_shapes=[pltpu.VMEM((SUBLANES, LANES), F32)],
        compiler_params=pltpu.CompilerParams(
            dimension_semantics=("arbitrary", "arbitrary"), vmem_limit_bytes=VMEM_LIMIT),
        name="inproj",
    )(x, g, w_all, w_kt, bf128, tril)


def _softmax_update(s, m, acc, v_aug):
    m_new = jnp.maximum(m, jnp.max(s, axis=-1, keepdims=True))
    p = jnp.exp(s - m_new).astype(BF16)
    acc_new = jnp.exp(m - m_new) * acc + jnp.dot(p, v_aug, preferred_element_type=F32)
    return m_new, acc_new


def _fox_kernel(q_ref, bq_ref, kt_ref, bkt_ref, v0_ref, v1_ref, o_ref):
    tq = q_ref.shape[0]
    pair = pl.program_id(1)
    qi = pl.program_id(2)
    lane = lax.broadcasted_iota(jnp.int32, (tq, LANES), 1)
    qf = q_ref[...].astype(F32)
    bqf = bq_ref[...].astype(F32)
    qa = []
    for hh in range(2):
        head = 2 * pair + hh
        qm = jnp.where((lane >= FOX_HEAD_DIM * hh) & (lane < FOX_HEAD_DIM * (hh + 1)), qf, 0.0)
        bm = jnp.where(((lane & 7) == head) & (lane < 56), bqf, 0.0)
        qa.append(jnp.concatenate([qm, bm], axis=1).astype(BF16))
    q_all = jnp.concatenate(qa, axis=0)
    v_refs = (v0_ref, v1_ref)

    def step(j, carry, causal):
        ks = pl.multiple_of(j * tq, tq)
        kt = jnp.concatenate([kt_ref[:, pl.ds(ks, tq)], bkt_ref[:, pl.ds(ks, tq)]], axis=0)
        s_all = jnp.dot(q_all, kt, preferred_element_type=F32)
        out = []
        for hh in range(2):
            s = s_all[hh * tq:(hh + 1) * tq]
            if causal:
                row = lax.broadcasted_iota(jnp.int32, s.shape, 0)
                col = lax.broadcasted_iota(jnp.int32, s.shape, 1)
                s = jnp.where(col <= row, s, -jnp.inf)
            m, acc = carry[hh]
            out.append(_softmax_update(s, m, acc, v_refs[hh][pl.ds(ks, tq), :]))
        return tuple(out)

    init = tuple((jnp.full((tq, 1), NEG_BIG, F32), jnp.zeros((tq, LANES), F32)) for _ in range(2))
    carry = lax.fori_loop(0, qi, lambda j, c: step(j, c, False), init)
    (_, acc0), (_, acc1) = step(qi, carry, True)
    o0 = acc0 / pltpu.roll(acc0, FOX_HEAD_DIM, 1)
    o1 = pltpu.roll(acc1, FOX_HEAD_DIM, 1) / acc1
    o_ref[...] = jnp.where(lane < FOX_HEAD_DIM, o0, o1).astype(BF16)


def _fox(qa, bq, kta, bkt, vaa, tq):
    b, s, _ = qa.shape
    npair = FOX_HEADS // 2
    return pl.pallas_call(
        _fox_kernel,
        grid=(b, npair, s // tq),
        in_specs=[
            pl.BlockSpec((None, tq, LANES), lambda bi, p, i: (bi, i, p)),
            pl.BlockSpec((None, tq, LANES), lambda bi, p, i: (bi, i, 0)),
            pl.BlockSpec((None, LANES, s), lambda bi, p, i: (bi, p, 0)),
            pl.BlockSpec((None, LANES, s), lambda bi, p, i: (bi, 0, 0)),
            pl.BlockSpec((None, s, LANES), lambda bi, p, i: (bi, 0, 2 * p)),
            pl.BlockSpec((None, s, LANES), lambda bi, p, i: (bi, 0, 2 * p + 1)),
        ],
        out_specs=pl.BlockSpec((None, tq, LANES), lambda bi, p, i: (bi, i, p)),
        out_shape=jax.ShapeDtypeStruct((b, s, FOX_WIDTH), BF16),
        compiler_params=pltpu.CompilerParams(
            dimension_semantics=("parallel", "parallel", "arbitrary"), vmem_limit_bytes=VMEM_LIMIT),
        name="fox",
    )(qa, bq, kta, bkt, vaa, vaa)


def _diff_kernel(q_ref, pq_ref, kt_ref, pkt_ref, v_ref, slope_ref, lam_ref, g_ref, o_ref, *, out_scale,
                 lambda_init):
    tq = q_ref.shape[0]
    qi = pl.program_id(2)
    lane = lax.broadcasted_iota(jnp.int32, (tq, LANES), 1)
    qf = q_ref[...].astype(F32)
    pq = pq_ref[...]
    q1 = jnp.where(lane < DIFF_HALF_DIM, qf, 0.0).astype(BF16)
    q2 = jnp.where(lane >= DIFF_HALF_DIM, qf, 0.0).astype(BF16)
    q_all = jnp.concatenate([jnp.concatenate([q1, pq], axis=1),
                             jnp.concatenate([q2, pq], axis=1)], axis=0)
    ones = jnp.ones((tq, LANES), BF16)

    def step(j, carry, diagonal):
        ks = pl.multiple_of(j * tq, tq)
        kt = jnp.concatenate([kt_ref[:, pl.ds(ks, tq)], pkt_ref[:, pl.ds(ks, tq)]], axis=0)
        s_all = jnp.dot(q_all, kt, preferred_element_type=F32)
        v_aug = jnp.concatenate([v_ref[pl.ds(ks, tq), :], ones], axis=1)
        if diagonal:
            row = lax.broadcasted_iota(jnp.int32, (tq, tq), 0)
            col = lax.broadcasted_iota(jnp.int32, (tq, tq), 1)
            fix = jnp.where((col // CHUNK) <= (row // CHUNK),
                            (-2.0 * slope_ref[0:1, 0:1]) * jnp.maximum(col - row, 0).astype(F32), -jnp.inf)
        out = []
        for t in range(2):
            s = s_all[t * tq:(t + 1) * tq]
            if diagonal:
                s = s + fix
            m, acc = carry[t]
            out.append(_softmax_update(s, m, acc, v_aug))
        return tuple(out)

    init = tuple((jnp.full((tq, 1), NEG_BIG, F32), jnp.zeros((tq, 2 * LANES), F32)) for _ in range(2))
    carry = lax.fori_loop(0, qi, lambda j, c: step(j, c, False), init)
    (_, acc1), (_, acc2) = step(qi, carry, True)

    lp = lam_ref[...]
    lam = (jnp.exp(jnp.sum(lp[0:1] * lp[1:2], axis=-1, keepdims=True))
           - jnp.exp(jnp.sum(lp[2:3] * lp[3:4], axis=-1, keepdims=True)) + lambda_init)
    o = acc1[:, :LANES] / acc1[:, LANES:] - lam * (acc2[:, :LANES] / acc2[:, LANES:])
    ms = jnp.mean(o * o, axis=-1, keepdims=True)
    o = o * lax.rsqrt(ms + RMS_EPS) * g_ref[...]
    o_ref[...] = (o * out_scale).astype(BF16)


def _diff(qb, posq, ktb, poskt, vb, slopes, lam_rows, subln_g, tq, lambda_init):
    b, s, _ = qb.shape
    kern = functools.partial(_diff_kernel, out_scale=1.0 - lambda_init, lambda_init=lambda_init)
    return pl.pallas_call(
        kern,
        grid=(b, DIFF_HEADS, s // tq),
        in_specs=[
            pl.BlockSpec((None, tq, LANES), lambda bi, h, i: (bi, i, h)),
            pl.BlockSpec((None, tq, LANES), lambda bi, h, i: (h, i, 0)),
            pl.BlockSpec((None, LANES, s), lambda bi, h, i: (bi, h, 0)),
            pl.BlockSpec((LANES, s), lambda bi, h, i: (0, 0)),
            pl.BlockSpec((None, s, LANES), lambda bi, h, i: (bi, 0, h)),
            pl.BlockSpec((None, SUBLANES, LANES), lambda bi, h, i: (h, 0, 0)),
            pl.BlockSpec((4, LANES), lambda bi, h, i: (0, 0)),
            pl.BlockSpec((1, LANES), lambda bi, h, i: (0, 0)),
        ],
        out_specs=pl.BlockSpec((None, tq, LANES), lambda bi, h, i: (bi, i, h)),
        out_shape=jax.ShapeDtypeStruct((b, s, DIFF_WIDTH), BF16),
        compiler_params=pltpu.CompilerParams(
            dimension_semantics=("parallel", "parallel", "arbitrary"), vmem_limit_bytes=VMEM_LIMIT),
        name="diff",
    )(qb, posq, ktb, poskt, vb, slopes, lam_rows, subln_g)


def _alibi_operands(s):
    pos = jnp.arange(s, dtype=jnp.int32)
    hi = (pos // 64).astype(F32)
    lo = (pos % 64).astype(F32)
    slopes = jnp.asarray([2.0 ** (-8.0 * (i + 1) / DIFF_HEADS) for i in range(DIFF_HEADS)], F32)
    zq = jnp.zeros((DIFF_HEADS, s, LANES), F32)
    sl = slopes[:, None]
    zq = zq.at[:, :, 0].set(-sl * 64.0 * hi[None]).at[:, :, 1].set(-sl * lo[None])
    zq = zq.at[:, :, 2].set(jnp.broadcast_to(sl * 64.0, (DIFF_HEADS, s)))
    zq = zq.at[:, :, 3].set(jnp.broadcast_to(sl, (DIFF_HEADS, s)))
    zk = jnp.zeros((LANES, s), F32)
    zk = zk.at[0].set(1.0).at[1].set(1.0).at[2].set(hi).at[3].set(lo)
    slope_tab = jnp.broadcast_to(slopes[:, None, None], (DIFF_HEADS, SUBLANES, LANES))
    return zq.astype(BF16), zk.astype(BF16), slope_tab


def _cmpx(xs, i, j):
    hi = jnp.maximum(xs[i], xs[j])
    lo = jnp.minimum(xs[i], xs[j])
    xs[i], xs[j] = hi, lo


def _bitonic_merge_desc(xs):
    n = len(xs)
    d = n // 2
    while d >= 1:
        for i in range(n):
            if (i & d) == 0:
                _cmpx(xs, i, i + d)
        d //= 2


def _sort_desc(xs):
    n = len(xs)
    k = 2
    while k <= n:
        d = k // 2
        while d >= 1:
            for i in range(n):
                l = i ^ d
                if l > i:
                    if (i & k) == 0:
                        _cmpx(xs, i, l)
                    else:
                        _cmpx(xs, l, i)
            d //= 2
        k *= 2


def _top16_desc(xs):
    xs = list(xs)
    _sort_desc(xs)
    for shift in (4, 2, 1):
        other = [pltpu.roll(xs[15 - i], shift, 0) for i in range(16)]
        xs = [jnp.maximum(xs[i], other[i]) for i in range(16)]
        _bitonic_merge_desc(xs)
    return xs


def _sublane_sum(x):
    for shift in (4, 2, 1):
        x = x + pltpu.roll(x, shift, 0)
    return x


def _sublane_min(x):
    for shift in (4, 2, 1):
        x = jnp.minimum(x, pltpu.roll(x, shift, 0))
    return x


def _route_kernel(oa_ref, ob_ref, x_ref, wo_ref, g_ref, wq_ref, keys_ref,
                  h_ref, hn_ref, e1_ref, e2_ref, kap_ref, sc_ref):
    tm = x_ref.shape[0]
    mixed = jnp.concatenate([oa_ref[...], ob_ref[...]], axis=1)
    h = x_ref[...] + jnp.dot(mixed, wo_ref[...], preferred_element_type=F32)
    h_ref[...] = h
    ms = jnp.mean(h * h, axis=-1, keepdims=True)
    hn = (h * lax.rsqrt(ms + RMS_EPS) * g_ref[...]).astype(BF16)
    hn_ref[...] = hn
    q = jnp.dot(hn, wq_ref[...], preferred_element_type=F32).astype(BF16)
    for hp in range(2 * PEER_HEADS):
        qs = q[:, hp * PEER_HALF_DIM:(hp + 1) * PEER_HALF_DIM]
        sc_ref[hp] = lax.dot_general(keys_ref[hp], qs, NT_DIMS, preferred_element_type=F32)

    sub = lax.broadcasted_iota(jnp.int32, (SUBLANES, tm), 0)
    invalid = jnp.full((SUBLANES, tm), -1.0, F32)

    def spread(vals):
        out = vals[SUBLANES - 1]
        for r in range(SUBLANES - 2, -1, -1):
            out = jnp.where(sub == r, vals[r], out)
        return out

    def head_body(hd, _):
        s1 = sc_ref[2 * hd]
        s2 = sc_ref[2 * hd + 1]
        groups = PEER_N_KEYS // SUBLANES
        a = _top16_desc([s1[g * SUBLANES:(g + 1) * SUBLANES, :] for g in range(groups)])
        b = _top16_desc([s2[g * SUBLANES:(g + 1) * SUBLANES, :] for g in range(groups)])
        ea = [jnp.exp(v - a[0]) for v in a]
        eb = [jnp.exp(v - b[0]) for v in b]
        eb_lo, eb_hi, ea_hi = spread(eb[:8]), spread(eb[8:]), spread(ea[8:])
        pairs = [(ea[0], eb_lo, SUBLANES), (ea[0], eb_hi, SUBLANES), (ea[1], eb_lo, SUBLANES)]
        pairs += [(ea[i], eb_lo, n) for i, n in ((2, 5), (3, 4), (4, 3), (5, 2), (6, 2), (7, 2))]
        pairs.append((ea_hi, eb[0], SUBLANES))
        cands = [x * y if n == SUBLANES else jnp.where(sub < n, x * y, invalid) for x, y, n in pairs]
        top = _top16_desc(cands + [invalid] * (16 - len(cands)))
        kap_u = top[PEER_TOPK - 1]
        z = jnp.zeros((SUBLANES, tm), F32)
        for c in cands:
            z = z + jnp.where(c >= kap_u, c, 0.0)
        inv_z = 1.0 / _sublane_sum(z)
        kap = jnp.full((SUBLANES, tm), jnp.inf, F32)
        for (x, y, _), c in zip(pairs, cands):
            kap = jnp.minimum(kap, jnp.where(c >= kap_u, x * (y * inv_z), jnp.inf))
        e1_ref[hd] = jnp.exp(s1 - a[0][0:1, :])
        e2_ref[hd] = jnp.exp(s2 - b[0][0:1, :]) * inv_z[0:1, :]
        kap_ref[hd] = _sublane_min(kap)
        return 0

    lax.fori_loop(0, PEER_HEADS, head_body, 0)


def _route(oa, ob, x2, wo, g2, wq, keys, tm):
    t, d = x2.shape
    nq = wq.shape[1]
    row = lambda i: (i, 0)
    const2 = lambda i: (0, 0)
    tok3 = lambda i: (0, 0, i)
    score_spec = pl.BlockSpec((PEER_HEADS, PEER_N_KEYS, tm), tok3)
    score_shape = jax.ShapeDtypeStruct((PEER_HEADS, PEER_N_KEYS, t), F32)
    return pl.pallas_call(
        _route_kernel,
        grid=(t // tm,),
        in_specs=[
            pl.BlockSpec((tm, FOX_WIDTH), row),
            pl.BlockSpec((tm, DIFF_WIDTH), row),
            pl.BlockSpec((tm, d), row),
            pl.BlockSpec((d, d), const2),
            pl.BlockSpec((1, d), const2),
            pl.BlockSpec((d, nq), const2),
            pl.BlockSpec((2 * PEER_HEADS, PEER_N_KEYS, PEER_HALF_DIM), lambda i: (0, 0, 0)),
        ],
        out_specs=[
            pl.BlockSpec((tm, d), row),
            pl.BlockSpec((tm, d), row),
            score_spec, score_spec,
            pl.BlockSpec((PEER_HEADS, SUBLANES, tm), tok3),
        ],
        out_shape=[
            jax.ShapeDtypeStruct((t, d), F32),
            jax.ShapeDtypeStruct((t, d), BF16),
            score_shape, score_shape,
            jax.ShapeDtypeStruct((PEER_HEADS, SUBLANES, t), F32),
        ],
        scratch_shapes=[pltpu.VMEM((2 * PEER_HEADS, PEER_N_KEYS, tm), F32)],
        compiler_params=pltpu.CompilerParams(
            dimension_semantics=("parallel",), vmem_limit_bytes=VMEM_LIMIT),
        name="route",
    )(oa, ob, x2, wo, g2, wq, keys)


def _gelu(x):
    return 0.5 * x * (1.0 + lax.erf(x * (1.0 / math.sqrt(2.0))))


def _peer_kernel(u_ref, x_ref, vt_ref, e1_ref, e2_ref, kap_ref, h_ref, g_ref,
                 o_ref, acc_ref, ht_ref, wact_ref, *, final_norm):
    j = pl.program_id(1)
    rows = u_ref.shape[0]
    n_a = rows // PEER_N_KEYS
    tt = x_ref.shape[0]

    @pl.when(j == 0)
    def _():
        acc_ref[...] = jnp.zeros_like(acc_ref)

    ht_ref[...] = lax.dot_general(u_ref[...], x_ref[...], NT_DIMS, preferred_element_type=F32)

    sub = lax.broadcasted_iota(jnp.int32, (n_a, LANES), 0)

    def a_body(al, _):
        r0 = pl.multiple_of(al * PEER_N_KEYS, PEER_N_KEYS)
        pick = sub == al
        for lc in range(tt // LANES):
            ls = slice(lc * LANES, (lc + 1) * LANES)
            act = _gelu(ht_ref[pl.ds(r0, PEER_N_KEYS), ls])
            w = jnp.zeros((PEER_N_KEYS, LANES), F32)
            for hd in range(PEER_HEADS):
                e1b = jnp.sum(jnp.where(pick, e1_ref[hd, :, ls], 0.0), axis=0, keepdims=True)
                gate = e1b * e2_ref[hd, :, ls]
                w = w + jnp.where(gate >= kap_ref[hd, 0:1, ls], gate, 0.0)
            wact_ref[pl.ds(r0, PEER_N_KEYS), ls] = (act * w).astype(BF16)
        return 0

    lax.fori_loop(0, n_a, a_body, 0)
    acc_ref[...] += jnp.dot(vt_ref[...], wact_ref[...], preferred_element_type=F32)

    @pl.when(j == pl.num_programs(1) - 1)
    def _():
        o = acc_ref[...].T + h_ref[...]
        if final_norm:
            ms = jnp.mean(o * o, axis=-1, keepdims=True)
            o = o * lax.rsqrt(ms + RMS_EPS) * g_ref[...]
        o_ref[...] = o


def _peer(u, hn, vt, e1, e2, kap, h, g, tt, rows, final_norm):
    t, d = hn.shape
    ne = u.shape[0]
    tok3 = lambda i, j: (0, 0, i)
    n_a = rows // PEER_N_KEYS
    assert n_a % SUBLANES == 0
    score_spec = pl.BlockSpec((PEER_HEADS, PEER_N_KEYS, tt), tok3)
    first_spec = pl.BlockSpec((PEER_HEADS, n_a, tt), lambda i, j: (0, j, i))
    return pl.pallas_call(
        functools.partial(_peer_kernel, final_norm=final_norm),
        grid=(t // tt, ne // rows),
        in_specs=[
            pl.BlockSpec((rows, d), lambda i, j: (j, 0)),
            pl.BlockSpec((tt, d), lambda i, j: (i, 0)),
            pl.BlockSpec((d, rows), lambda i, j: (0, j)),
            first_spec, score_spec,
            pl.BlockSpec((PEER_HEADS, SUBLANES, tt), tok3),
            pl.BlockSpec((tt, d), lambda i, j: (i, 0)),
            pl.BlockSpec((1, d), lambda i, j: (0, 0)),
        ],
        out_specs=pl.BlockSpec((tt, d), lambda i, j: (i, 0)),
        out_shape=jax.ShapeDtypeStruct((t, d), F32),
        scratch_shapes=[
            pltpu.VMEM((d, tt), F32),
            pltpu.VMEM((rows, tt), F32),
            pltpu.VMEM((rows, tt), BF16),
        ],
        compiler_params=pltpu.CompilerParams(
            dimension_semantics=("parallel", "arbitrary"), vmem_limit_bytes=VMEM_LIMIT),
        name="peer",
    )(u, hn, vt, e1, e2, kap, h, g)


def _largest_tile(n, cap):
    t = cap
    while n % t:
        t //= 2
    return t


def kernel(x, norm1_g, w_in, b_f, lambda_q1, lambda_k1, lambda_q2, lambda_k2, subln_g, w_out,
           norm2_g, peer_w_q, peer_sub_keys, peer_u, peer_v, final_g):
    b, s, d = x.shape
    depth = w_in.shape[0]
    t = b * s
    tile_s = _largest_tile(s, 512)
    tile_in = _largest_tile(s, 512)
    tile_route = _largest_tile(t, 256)
    tile_peer = _largest_tile(t, 512)
    rows_peer = 1024

    posq, poskt, slope_tab = _alibi_operands(s)
    h = x
    for layer in range(depth):
        w = w_in[layer]
        o = 0
        parts = {}
        for name, width in (("qa", FOX_WIDTH), ("ka", FOX_WIDTH), ("va", FOX_WIDTH), ("f", FOX_HEADS),
                            ("qb", DIFF_WIDTH), ("kb", DIFF_WIDTH), ("vb", DIFF_WIDTH)):
            parts[name] = w[:, o:o + width]
            o += width
        gate = jnp.zeros((d, LANES), F32)
        bf128 = jnp.zeros((1, LANES), F32)
        for g0 in GATE_GROUPS:
            gate = gate.at[:, g0:g0 + FOX_HEADS].set(parts["f"])
            bf128 = bf128.at[0, g0:g0 + FOX_HEADS].set(b_f[layer])
        va_pad = jnp.pad(parts["va"].reshape(d, FOX_HEADS, FOX_HEAD_DIM),
                         ((0, 0), (0, 0), (0, FOX_HEAD_DIM))).reshape(d, 2 * FOX_WIDTH)
        w_all = jnp.concatenate([parts["qa"], va_pad, parts["qb"], parts["vb"], gate], axis=1).astype(BF16)
        w_kt = jnp.concatenate([parts["ka"], parts["kb"]], axis=1).T.astype(BF16)

        qa, bq, kta, bkt, vaa, qb, ktb, vb = _inproj(h, norm1_g[layer][None, :], w_all, w_kt, bf128, tile_in)
        out_a = _fox(qa, bq, kta, bkt, vaa, tile_s)

        lambda_init = 0.8 - 0.6 * math.exp(-0.3 * layer)
        lam_rows = jnp.zeros((4, LANES), F32)
        for r, p in enumerate((lambda_q1, lambda_k1, lambda_q2, lambda_k2)):
            lam_rows = lam_rows.at[r, :DIFF_HALF_DIM].set(p[layer].astype(F32))
        out_b = _diff(qb, posq, ktb, poskt, vb, slope_tab, lam_rows, subln_g[layer][None, :].astype(F32),
                      tile_s, lambda_init)

        keys = peer_sub_keys[layer].reshape(2 * PEER_HEADS, PEER_N_KEYS, PEER_HALF_DIM).astype(BF16)
        h2, hn, e1, e2, kap = _route(
            out_a.reshape(t, FOX_WIDTH), out_b.reshape(t, DIFF_WIDTH), h.reshape(t, d),
            w_out[layer].astype(BF16), norm2_g[layer][None, :], peer_w_q[layer].astype(BF16), keys,
            tile_route)

        last = layer == depth - 1
        h = _peer(peer_u[layer].astype(BF16), hn, peer_v[layer].T.astype(BF16), e1, e2, kap,
                  h2, final_g[None, :], tile_peer, rows_peer, last).reshape(b, s, d)
    return h
```

```python
import functools
import math

import jax
import jax.numpy as jnp
from jax import lax
from jax.experimental import pallas as pl
from jax.experimental.pallas import tpu as pltpu

F32 = jnp.float32
BF16 = jnp.bfloat16

D_MODEL = 1024
FOX_HEADS = 8
FOX_HEAD_DIM = 64
FOX_WIDTH = FOX_HEADS * FOX_HEAD_DIM
DIFF_HEADS = 4
DIFF_HALF_DIM = 64
DIFF_HEAD_DIM = 2 * DIFF_HALF_DIM
DIFF_WIDTH = DIFF_HEADS * DIFF_HEAD_DIM
CHUNK = 64
PEER_HEADS = 8
PEER_N_KEYS = 128
PEER_N_EXPERTS = PEER_N_KEYS * PEER_N_KEYS
PEER_HALF_DIM = 128
PEER_TOPK = 16
RMS_EPS = 1e-6

LANES = 128
SUBLANES = 8
NEG_BIG = -1e30
VMEM_LIMIT = 56 * 1024 * 1024

GATE_GROUPS = (0, 8, 16, 32, 40, 48)

NT_DIMS = (((1,), (1,)), ((), ()))


def _split3(v):
    hi = v.astype(BF16)
    r1 = v - hi.astype(F32)
    mid = r1.astype(BF16)
    lo = (r1 - mid.astype(F32)).astype(BF16)
    return hi, mid, lo


def _inproj_kernel(x_ref, g_ref, w_ref, wkt_ref, bf_ref, tril_ref,
                   qa_ref, bq_ref, kta_ref, bkt_ref, va_ref, qb_ref, ktb_ref, vb_ref, carry_ref):
    tm = x_ref.shape[0]

    @pl.when(pl.program_id(1) == 0)
    def _():
        carry_ref[...] = jnp.zeros_like(carry_ref)

    x = x_ref[...]
    ms = jnp.mean(x * x, axis=-1, keepdims=True)
    xb = (x * lax.rsqrt(ms + RMS_EPS) * g_ref[...]).astype(BF16)
    proj = jnp.dot(xb, w_ref[...], preferred_element_type=F32)
    kt = lax.dot_general(wkt_ref[...], xb, NT_DIMS, preferred_element_type=F32)

    o_va = FOX_WIDTH
    o_qb = o_va + 2 * FOX_WIDTH
    o_vb = o_qb + DIFF_WIDTH
    o_gate = o_vb + DIFF_WIDTH
    qa_ref[...] = (proj[:, 0:FOX_WIDTH] * 0.125).astype(BF16)
    qb_ref[...] = (proj[:, o_qb:o_qb + DIFF_WIDTH] * 0.125).astype(BF16)
    vb_ref[...] = proj[:, o_vb:o_vb + DIFF_WIDTH].astype(BF16)
    kta_ref[...] = kt[0:FOX_WIDTH, :].astype(BF16)
    ktb_ref[...] = kt[FOX_WIDTH:FOX_WIDTH + DIFF_WIDTH, :].astype(BF16)
    va = proj[:, o_va:o_qb]
    vlane = lax.broadcasted_iota(jnp.int32, va.shape, 1)
    va_ref[...] = jnp.where((vlane & FOX_HEAD_DIM) != 0, 1.0, va).astype(BF16)

    z = proj[:, o_gate:o_gate + LANES] + bf_ref[...]
    logf = jnp.minimum(z, 0.0) - jnp.log1p(jnp.exp(-jnp.abs(z)))
    tril = tril_ref[...]
    hi, mid, lo = _split3(logf)
    c = (jnp.dot(tril, hi, preferred_element_type=F32)
         + jnp.dot(tril, mid, preferred_element_type=F32)
         + jnp.dot(tril, lo, preferred_element_type=F32)) + carry_ref[0:1, :]
    carry_ref[...] = jnp.broadcast_to(c[tm - 1:tm, :], carry_ref.shape)

    chi, cmid, clo = _split3(c)
    chi, cmid, clo = chi.astype(F32), cmid.astype(F32), clo.astype(F32)
    lane = lax.broadcasted_iota(jnp.int32, c.shape, 1)
    ones_hi = (lane >= 32) & (lane < 56)
    bq = jnp.where(lane < 8, chi, jnp.where(lane < 16, cmid, jnp.where(lane < 24, clo,
                   jnp.where(ones_hi, 1.0, 0.0))))
    bk = jnp.where(lane < 24, 1.0, jnp.where(lane < 32, 0.0, jnp.where(lane < 40, -chi,
                   jnp.where(lane < 48, -cmid, jnp.where(lane < 56, -clo, 0.0)))))
    bq_ref[...] = bq.astype(BF16)
    bkt_ref[...] = bk.T.astype(BF16)


def _inproj(x, g, w_all, w_kt, bf128, tm):
    b, s, d = x.shape
    ncol = w_all.shape[1]
    nkt = w_kt.shape[0]
    tril = jnp.tril(jnp.ones((tm, tm), F32)).astype(BF16)
    row = lambda bi, si: (bi, si, 0)
    col = lambda bi, si: (bi, 0, si)
    const = lambda bi, si: (0, 0)
    shapes = [
        ((b, s, FOX_WIDTH), pl.BlockSpec((None, tm, FOX_WIDTH), row)),
        ((b, s, LANES), pl.BlockSpec((None, tm, LANES), row)),
        ((b, FOX_WIDTH, s), pl.BlockSpec((None, FOX_WIDTH, tm), col)),
        ((b, LANES, s), pl.BlockSpec((None, LANES, tm), col)),
        ((b, s, 2 * FOX_WIDTH), pl.BlockSpec((None, tm, 2 * FOX_WIDTH), row)),
        ((b, s, DIFF_WIDTH), pl.BlockSpec((None, tm, DIFF_WIDTH), row)),
        ((b, DIFF_WIDTH, s), pl.BlockSpec((None, DIFF_WIDTH, tm), col)),
        ((b, s, DIFF_WIDTH), pl.BlockSpec((None, tm, DIFF_WIDTH), row)),
    ]
    return pl.pallas_call(
        _inproj_kernel,
        grid=(b, s // tm),
        in_specs=[
            pl.BlockSpec((None, tm, d), row),
            pl.BlockSpec((1, d), const),
            pl.BlockSpec((d, ncol), const),
            pl.BlockSpec((nkt, d), const),
            pl.BlockSpec((1, LANES), const),
            pl.BlockSpec((tm, tm), const),
        ],
        out_specs=[spec for _, spec in shapes],
        out_shape=[jax.ShapeDtypeStruct(shape, BF16) for shape, _ in shapes],
        scratch_shapes=[pltpu.VMEM((SUBLANES, LANES), F32)],
        compiler_params=pltpu.CompilerParams(
            dimension_semantics=("arbitrary", "arbitrary"), vmem_limit_bytes=VMEM_LIMIT),
        name="inproj",
    )(x, g, w_all, w_kt, bf128, tril)


def _softmax_update(s, m, acc, v_aug):
    m_new = jnp.maximum(m, jnp.max(s, axis=-1, keepdims=True))
    p = jnp.exp(s - m_new).astype(BF16)
    acc_new = jnp.exp(m - m_new) * acc + jnp.dot(p, v_aug, preferred_element_type=F32)
    return m_new, acc_new


def _fox_kernel(q_ref, bq_ref, kt_ref, bkt_ref, v0_ref, v1_ref, o_ref):
    tq = q_ref.shape[0]
    pair = pl.program_id(1)
    qi = pl.program_id(2)
    lane = lax.broadcasted_iota(jnp.int32, (tq, LANES), 1)
    qf = q_ref[...].astype(F32)
    bqf = bq_ref[...].astype(F32)
    qa = []
    for hh in range(2):
        head = 2 * pair + hh
        qm = jnp.where((lane >= FOX_HEAD_DIM * hh) & (lane < FOX_HEAD_DIM * (hh + 1)), qf, 0.0)
        bm = jnp.where(((lane & 7) == head) & (lane < 56), bqf, 0.0)
        qa.append(jnp.concatenate([qm, bm], axis=1).astype(BF16))
    q_all = jnp.concatenate(qa, axis=0)
    v_refs = (v0_ref, v1_ref)

    def step(j, carry, causal):
        ks = pl.multiple_of(j * tq, tq)
        kt = jnp.concatenate([kt_ref[:, pl.ds(ks, tq)], bkt_ref[:, pl.ds(ks, tq)]], axis=0)
        s_all = jnp.dot(q_all, kt, preferred_element_type=F32)
        out = []
        for hh in range(2):
            s = s_all[hh * tq:(hh + 1) * tq]
            if causal:
                row = lax.broadcasted_iota(jnp.int32, s.shape, 0)
                col = lax.broadcasted_iota(jnp.int32, s.shape, 1)
                s = jnp.where(col <= row, s, -jnp.inf)
            m, acc = carry[hh]
            out.append(_softmax_update(s, m, acc, v_refs[hh][pl.ds(ks, tq), :]))
        return tuple(out)

    init = tuple((jnp.full((tq, 1), NEG_BIG, F32), jnp.zeros((tq, LANES), F32)) for _ in range(2))
    carry = lax.fori_loop(0, qi, lambda j, c: step(j, c, False), init)
    (_, acc0), (_, acc1) = step(qi, carry, True)
    o0 = acc0 / pltpu.roll(acc0, FOX_HEAD_DIM, 1)
    o1 = pltpu.roll(acc1, FOX_HEAD_DIM, 1) / acc1
    o_ref[...] = jnp.where(lane < FOX_HEAD_DIM, o0, o1).astype(BF16)


def _fox(qa, bq, kta, bkt, vaa, tq):
    b, s, _ = qa.shape
    npair = FOX_HEADS // 2
    return pl.pallas_call(
        _fox_kernel,
        grid=(b, npair, s // tq),
        in_specs=[
            pl.BlockSpec((None, tq, LANES), lambda bi, p, i: (bi, i, p)),
            pl.BlockSpec((None, tq, LANES), lambda bi, p, i: (bi, i, 0)),
            pl.BlockSpec((None, LANES, s), lambda bi, p, i: (bi, p, 0)),
            pl.BlockSpec((None, LANES, s), lambda bi, p, i: (bi, 0, 0)),
            pl.BlockSpec((None, s, LANES), lambda bi, p, i: (bi, 0, 2 * p)),
            pl.BlockSpec((None, s, LANES), lambda bi, p, i: (bi, 0, 2 * p + 1)),
        ],
        out_specs=pl.BlockSpec((None, tq, LANES), lambda bi, p, i: (bi, i, p)),
        out_shape=jax.ShapeDtypeStruct((b, s, FOX_WIDTH), BF16),
        compiler_params=pltpu.CompilerParams(
            dimension_semantics=("parallel", "parallel", "arbitrary"), vmem_limit_bytes=VMEM_LIMIT),
        name="fox",
    )(qa, bq, kta, bkt, vaa, vaa)


def _diff_kernel(q_ref, pq_ref, kt_ref, pkt_ref, v_ref, slope_ref, lam_ref, g_ref, o_ref, *, out_scale,
                 lambda_init):
    tq = q_ref.shape[0]
    qi = pl.program_id(2)
    lane = lax.broadcasted_iota(jnp.int32, (tq, LANES), 1)
    qf = q_ref[...].astype(F32)
    pq = pq_ref[...]
    q1 = jnp.where(lane < DIFF_HALF_DIM, qf, 0.0).astype(BF16)
    q2 = jnp.where(lane >= DIFF_HALF_DIM, qf, 0.0).astype(BF16)
    q_all = jnp.concatenate([jnp.concatenate([q1, pq], axis=1),
                             jnp.concatenate([q2, pq], axis=1)], axis=0)
    ones = jnp.ones((tq, LANES), BF16)

    def step(j, carry, diagonal):
        ks = pl.multiple_of(j * tq, tq)
        kt = jnp.concatenate([kt_ref[:, pl.ds(ks, tq)], pkt_ref[:, pl.ds(ks, tq)]], axis=0)
        s_all = jnp.dot(q_all, kt, preferred_element_type=F32)
        v_aug = jnp.concatenate([v_ref[pl.ds(ks, tq), :], ones], axis=1)
        if diagonal:
            row = lax.broadcasted_iota(jnp.int32, (tq, tq), 0)
            col = lax.broadcasted_iota(jnp.int32, (tq, tq), 1)
            fix = jnp.where((col // CHUNK) <= (row // CHUNK),
                            (-2.0 * slope_ref[0:1, 0:1]) * jnp.maximum(col - row, 0).astype(F32), -jnp.inf)
        out = []
        for t in range(2):
            s = s_all[t * tq:(t + 1) * tq]
            if diagonal:
                s = s + fix
            m, acc = carry[t]
            out.append(_softmax_update(s, m, acc, v_aug))
        return tuple(out)

    init = tuple((jnp.full((tq, 1), NEG_BIG, F32), jnp.zeros((tq, 2 * LANES), F32)) for _ in range(2))
    carry = lax.fori_loop(0, qi, lambda j, c: step(j, c, False), init)
    (_, acc1), (_, acc2) = step(qi, carry, True)

    lp = lam_ref[...]
    lam = (jnp.exp(jnp.sum(lp[0:1] * lp[1:2], axis=-1, keepdims=True))
           - jnp.exp(jnp.sum(lp[2:3] * lp[3:4], axis=-1, keepdims=True)) + lambda_init)
    o = acc1[:, :LANES] / acc1[:, LANES:] - lam * (acc2[:, :LANES] / acc2[:, LANES:])
    ms = jnp.mean(o * o, axis=-1, keepdims=True)
    o = o * lax.rsqrt(ms + RMS_EPS) * g_ref[...]
    o_ref[...] = (o * out_scale).astype(BF16)


def _diff(qb, posq, ktb, poskt, vb, slopes, lam_rows, subln_g, tq, lambda_init):
    b, s, _ = qb.shape
    kern = functools.partial(_diff_kernel, out_scale=1.0 - lambda_init, lambda_init=lambda_init)
    return pl.pallas_call(
        kern,
        grid=(b, DIFF_HEADS, s // tq),
        in_specs=[
            pl.BlockSpec((None, tq, LANES), lambda bi, h, i: (bi, i, h)),
            pl.BlockSpec((None, tq, LANES), lambda bi, h, i: (h, i, 0)),
            pl.BlockSpec((None, LANES, s), lambda bi, h, i: (bi, h, 0)),
            pl.BlockSpec((LANES, s), lambda bi, h, i: (0, 0)),
            pl.BlockSpec((None, s, LANES), lambda bi, h, i: (bi, 0, h)),
            pl.BlockSpec((None, SUBLANES, LANES), lambda bi, h, i: (h, 0, 0)),
            pl.BlockSpec((4, LANES), lambda bi, h, i: (0, 0)),
            pl.BlockSpec((1, LANES), lambda bi, h, i: (0, 0)),
        ],
        out_specs=pl.BlockSpec((None, tq, LANES), lambda bi, h, i: (bi, i, h)),
        out_shape=jax.ShapeDtypeStruct((b, s, DIFF_WIDTH), BF16),
        compiler_params=pltpu.CompilerParams(
            dimension_semantics=("parallel", "parallel", "arbitrary"), vmem_limit_bytes=VMEM_LIMIT),
        name="diff",
    )(qb, posq, ktb, poskt, vb, slopes, lam_rows, subln_g)


def _alibi_operands(s):
    pos = jnp.arange(s, dtype=jnp.int32)
    hi = (pos // 64).astype(F32)
    lo = (pos % 64).astype(F32)
    slopes = jnp.asarray([2.0 ** (-8.0 * (i + 1) / DIFF_HEADS) for i in range(DIFF_HEADS)], F32)
    zq = jnp.zeros((DIFF_HEADS, s, LANES), F32)
    sl = slopes[:, None]
    zq = zq.at[:, :, 0].set(-sl * 64.0 * hi[None]).at[:, :, 1].set(-sl * lo[None])
    zq = zq.at[:, :, 2].set(jnp.broadcast_to(sl * 64.0, (DIFF_HEADS, s)))
    zq = zq.at[:, :, 3].set(jnp.broadcast_to(sl, (DIFF_HEADS, s)))
    zk = jnp.zeros((LANES, s), F32)
    zk = zk.at[0].set(1.0).at[1].set(1.0).at[2].set(hi).at[3].set(lo)
    slope_tab = jnp.broadcast_to(slopes[:, None, None], (DIFF_HEADS, SUBLANES, LANES))
    return zq.astype(BF16), zk.astype(BF16), slope_tab


def _cmpx(xs, i, j):
    hi = jnp.maximum(xs[i], xs[j])
    lo = jnp.minimum(xs[i], xs[j])
    xs[i], xs[j] = hi, lo


def _bitonic_merge_desc(xs):
    n = len(xs)
    d = n // 2
    while d >= 1:
        for i in range(n):
            if (i & d) == 0:
                _cmpx(xs, i, i + d)
        d //= 2


def _sort_desc(xs):
    n = len(xs)
    k = 2
    while k <= n:
        d = k // 2
        while d >= 1:
            for i in range(n):
                l = i ^ d
                if l > i:
                    if (i & k) == 0:
                        _cmpx(xs, i, l)
                    else:
                        _cmpx(xs, l, i)
            d //= 2
        k *= 2


def _top16_desc(xs):
    xs = list(xs)
    _sort_desc(xs)
    for shift in (4, 2, 1):
        other = [pltpu.roll(xs[15 - i], shift, 0) for i in range(16)]
        xs = [jnp.maximum(xs[i], other[i]) for i in range(16)]
        _bitonic_merge_desc(xs)
    return xs


def _sublane_sum(x):
    for shift in (4, 2, 1):
        x = x + pltpu.roll(x, shift, 0)
    return x


def _sublane_min(x):
    for shift in (4, 2, 1):
        x = jnp.minimum(x, pltpu.roll(x, shift, 0))
    return x


def _route_kernel(oa_ref, ob_ref, x_ref, wo_ref, g_ref, wq_ref, keys_ref,
                  h_ref, hnt_ref, e1_ref, e2_ref, kap_ref, sc_ref):
    tm = x_ref.shape[0]
    mixed = jnp.concatenate([oa_ref[...], ob_ref[...]], axis=1)
    h = x_ref[...] + jnp.dot(mixed, wo_ref[...], preferred_element_type=F32)
    h_ref[...] = h
    ms = jnp.mean(h * h, axis=-1, keepdims=True)
    hn_f = h * lax.rsqrt(ms + RMS_EPS) * g_ref[...]
    hn = hn_f.astype(BF16)
    hnt_ref[...] = hn_f.T.astype(BF16)
    q = jnp.dot(hn, wq_ref[...], preferred_element_type=F32).astype(BF16)
    for hp in range(2 * PEER_HEADS):
        qs = q[:, hp * PEER_HALF_DIM:(hp + 1) * PEER_HALF_DIM]
        sc_ref[hp] = lax.dot_general(keys_ref[hp], qs, NT_DIMS, preferred_element_type=F32)

    sub = lax.broadcasted_iota(jnp.int32, (SUBLANES, tm), 0)
    invalid = jnp.full((SUBLANES, tm), -1.0, F32)

    def spread(vals):
        out = vals[SUBLANES - 1]
        for r in range(SUBLANES - 2, -1, -1):
            out = jnp.where(sub == r, vals[r], out)
        return out

    def head_body(hd, _):
        s1 = sc_ref[2 * hd]
        s2 = sc_ref[2 * hd + 1]
        groups = PEER_N_KEYS // SUBLANES
        a = _top16_desc([s1[g * SUBLANES:(g + 1) * SUBLANES, :] for g in range(groups)])
        b = _top16_desc([s2[g * SUBLANES:(g + 1) * SUBLANES, :] for g in range(groups)])
        ea = [jnp.exp(v - a[0]) for v in a]
        eb = [jnp.exp(v - b[0]) for v in b]
        eb_lo, eb_hi, ea_hi = spread(eb[:8]), spread(eb[8:]), spread(ea[8:])
        pairs = [(ea[0], eb_lo, SUBLANES), (ea[0], eb_hi, SUBLANES), (ea[1], eb_lo, SUBLANES)]
        pairs += [(ea[i], eb_lo, n) for i, n in ((2, 5), (3, 4), (4, 3), (5, 2), (6, 2), (7, 2))]
        pairs.append((ea_hi, eb[0], SUBLANES))
        cands = [x * y if n == SUBLANES else jnp.where(sub < n, x * y, invalid) for x, y, n in pairs]
        top = _top16_desc(cands + [invalid] * (16 - len(cands)))
        kap_u = top[PEER_TOPK - 1]
        z = jnp.zeros((SUBLANES, tm), F32)
        for c in cands:
            z = z + jnp.where(c >= kap_u, c, 0.0)
        inv_z = 0.5 / _sublane_sum(z)
        kap = jnp.full((SUBLANES, tm), jnp.inf, F32)
        for (x, y, _), c in zip(pairs, cands):
            kap = jnp.minimum(kap, jnp.where(c >= kap_u, x * (y * inv_z), jnp.inf))
        e1_ref[hd] = jnp.exp(s1 - a[0][0:1, :])
        e2_ref[hd] = jnp.exp(s2 - b[0][0:1, :]) * inv_z[0:1, :]
        kap_ref[hd] = _sublane_min(kap)
        return 0

    lax.fori_loop(0, PEER_HEADS, head_body, 0)


def _route(oa, ob, x2, wo, g2, wq, keys, tm):
    t, d = x2.shape
    nq = wq.shape[1]
    row = lambda i: (i, 0)
    const2 = lambda i: (0, 0)
    tok3 = lambda i: (0, 0, i)
    score_spec = pl.BlockSpec((PEER_HEADS, PEER_N_KEYS, tm), tok3)
    score_shape = jax.ShapeDtypeStruct((PEER_HEADS, PEER_N_KEYS, t), F32)
    return pl.pallas_call(
        _route_kernel,
        grid=(t // tm,),
        in_specs=[
            pl.BlockSpec((tm, FOX_WIDTH), row),
            pl.BlockSpec((tm, DIFF_WIDTH), row),
            pl.BlockSpec((tm, d), row),
            pl.BlockSpec((d, d), const2),
            pl.BlockSpec((1, d), const2),
            pl.BlockSpec((d, nq), const2),
            pl.BlockSpec((2 * PEER_HEADS, PEER_N_KEYS, PEER_HALF_DIM), lambda i: (0, 0, 0)),
        ],
        out_specs=[
            pl.BlockSpec((tm, d), row),
            pl.BlockSpec((d, tm), lambda i: (0, i)),
            score_spec, score_spec,
            pl.BlockSpec((PEER_HEADS, SUBLANES, tm), tok3),
        ],
        out_shape=[
            jax.ShapeDtypeStruct((t, d), F32),
            jax.ShapeDtypeStruct((d, t), BF16),
            score_shape, score_shape,
            jax.ShapeDtypeStruct((PEER_HEADS, SUBLANES, t), F32),
        ],
        scratch_shapes=[pltpu.VMEM((2 * PEER_HEADS, PEER_N_KEYS, tm), F32)],
        compiler_params=pltpu.CompilerParams(
            dimension_semantics=("parallel",), vmem_limit_bytes=VMEM_LIMIT),
        name="route",
    )(oa, ob, x2, wo, g2, wq, keys)


def _gelu2(x):
    return x * (1.0 + lax.erf(x * (1.0 / math.sqrt(2.0))))


MXU_TILE = 256
EW_ROWS = 32


def _peer_kernel(u0_ref, xt0_ref, un_ref, xtn_ref, vtp_ref, e1_ref, e2_ref, kap_ref, h_ref, g_ref,
                 o_ref, acc_ref, hta_ref, htb_ref, wa_ref, wb_ref, *, final_norm, tiles_per_block):
    n = pl.program_id(0)
    rows = un_ref.shape[0]
    tt = xtn_ref.shape[1]
    n_a = rows // PEER_N_KEYS
    col_tiles = tt // MXU_TILE
    assert (rows // MXU_TILE) * col_tiles == n_a

    @pl.when(n == 0)
    def _():
        acc_ref[...] = jnp.zeros_like(acc_ref)
        wb_ref[...] = jnp.zeros_like(wb_ref)
        hta_ref[...] = jnp.dot(u0_ref[...], xt0_ref[...], preferred_element_type=F32)

    def run(ht_cur, ht_nxt, w_cur, w_prev):
        for al in range(n_a):
            rt = pl.ds((al // col_tiles) * MXU_TILE, MXU_TILE)
            ct = pl.ds((al % col_tiles) * MXU_TILE, MXU_TILE)
            e1b = [e1_ref[hd, al:al + 1, :] for hd in range(PEER_HEADS)]
            kap = [kap_ref[hd, 0:1, :] for hd in range(PEER_HEADS)]
            for rc in range(PEER_N_KEYS // EW_ROWS):
                rs = slice(rc * EW_ROWS, (rc + 1) * EW_ROWS)
                rd = pl.ds(al * PEER_N_KEYS + rc * EW_ROWS, EW_ROWS)
                act = _gelu2(ht_cur[rd, :])
                w = None
                for hd in range(PEER_HEADS):
                    gate = e1b[hd] * e2_ref[hd, rs, :]
                    part = jnp.where(gate >= kap[hd], gate, 0.0)
                    w = part if w is None else w + part
                w_cur[rd, :] = (act * w).astype(BF16)
            ht_nxt[rt, ct] = jnp.dot(un_ref[rt, :], xtn_ref[:, ct], preferred_element_type=F32)
            acc_ref[rt, ct] += jnp.dot(vtp_ref[rt, :], w_prev[:, ct], preferred_element_type=F32)

    @pl.when(n % 2 == 0)
    def _():
        run(hta_ref, htb_ref, wa_ref, wb_ref)

    @pl.when(n % 2 == 1)
    def _():
        run(htb_ref, hta_ref, wb_ref, wa_ref)

    @pl.when((n % tiles_per_block == 0) & (n > 0))
    def _():
        o = acc_ref[...].T + h_ref[...]
        if final_norm:
            ms = jnp.mean(o * o, axis=-1, keepdims=True)
            o = o * lax.rsqrt(ms + RMS_EPS) * g_ref[...]
        o_ref[...] = o
        acc_ref[...] = jnp.zeros_like(acc_ref)


def _peer(u, hnt, vt, e1, e2, kap, h, g, tt, rows, final_norm):
    d, t = hnt.shape
    ne = u.shape[0]
    n_a = rows // PEER_N_KEYS
    assert n_a % SUBLANES == 0 and d == rows
    per_block = ne // rows
    n_blocks = t // tt
    n_tiles = per_block * n_blocks
    tok = lambda n: jnp.minimum(n // per_block, n_blocks - 1)
    tok_prev = lambda n: jnp.maximum(n - 1, 0) // per_block
    kern = functools.partial(_peer_kernel, final_norm=final_norm, tiles_per_block=per_block)
    return pl.pallas_call(
        kern,
        grid=(n_tiles + 1,),
        in_specs=[
            pl.BlockSpec((rows, d), lambda n: (0, 0)),
            pl.BlockSpec((d, tt), lambda n: (0, 0)),
            pl.BlockSpec((rows, d), lambda n: ((n + 1) % per_block, 0)),
            pl.BlockSpec((d, tt), lambda n: (0, tok(n + 1))),
            pl.BlockSpec((d, rows), lambda n: (0, (n + per_block - 1) % per_block)),
            pl.BlockSpec((PEER_HEADS, n_a, tt), lambda n: (0, n % per_block, tok(n))),
            pl.BlockSpec((PEER_HEADS, PEER_N_KEYS, tt), lambda n: (0, 0, tok(n))),
            pl.BlockSpec((PEER_HEADS, SUBLANES, tt), lambda n: (0, 0, tok(n))),
            pl.BlockSpec((tt, d), lambda n: (tok_prev(n), 0)),
            pl.BlockSpec((1, d), lambda n: (0, 0)),
        ],
        out_specs=pl.BlockSpec((tt, d), lambda n: (tok_prev(n), 0)),
        out_shape=jax.ShapeDtypeStruct((t, d), F32),
        scratch_shapes=[
            pltpu.VMEM((d, tt), F32),
            pltpu.VMEM((rows, tt), F32),
            pltpu.VMEM((rows, tt), F32),
            pltpu.VMEM((rows, tt), BF16),
            pltpu.VMEM((rows, tt), BF16),
        ],
        compiler_params=pltpu.CompilerParams(
            dimension_semantics=("arbitrary",), vmem_limit_bytes=VMEM_LIMIT),
        name="peer",
    )(u, hnt, u, hnt, vt, e1, e2, kap, h, g)


def _largest_tile(n, cap):
    t = cap
    while n % t:
        t //= 2
    return t


def kernel(x, norm1_g, w_in, b_f, lambda_q1, lambda_k1, lambda_q2, lambda_k2, subln_g, w_out,
           norm2_g, peer_w_q, peer_sub_keys, peer_u, peer_v, final_g):
    b, s, d = x.shape
    depth = w_in.shape[0]
    t = b * s
    tile_s = _largest_tile(s, 512)
    tile_in = _largest_tile(s, 512)
    tile_route = _largest_tile(t, 256)
    tile_peer = _largest_tile(t, 512)
    rows_peer = 1024

    posq, poskt, slope_tab = _alibi_operands(s)
    h = x
    for layer in range(depth):
        w = w_in[layer]
        o = 0
        parts = {}
        for name, width in (("qa", FOX_WIDTH), ("ka", FOX_WIDTH), ("va", FOX_WIDTH), ("f", FOX_HEADS),
                            ("qb", DIFF_WIDTH), ("kb", DIFF_WIDTH), ("vb", DIFF_WIDTH)):
            parts[name] = w[:, o:o + width]
            o += width
        gate = jnp.zeros((d, LANES), F32)
        bf128 = jnp.zeros((1, LANES), F32)
        for g0 in GATE_GROUPS:
            gate = gate.at[:, g0:g0 + FOX_HEADS].set(parts["f"])
            bf128 = bf128.at[0, g0:g0 + FOX_HEADS].set(b_f[layer])
        va_pad = jnp.pad(parts["va"].reshape(d, FOX_HEADS, FOX_HEAD_DIM),
                         ((0, 0), (0, 0), (0, FOX_HEAD_DIM))).reshape(d, 2 * FOX_WIDTH)
        w_all = jnp.concatenate([parts["qa"], va_pad, parts["qb"], parts["vb"], gate], axis=1).astype(BF16)
        w_kt = jnp.concatenate([parts["ka"], parts["kb"]], axis=1).T.astype(BF16)

        qa, bq, kta, bkt, vaa, qb, ktb, vb = _inproj(h, norm1_g[layer][None, :], w_all, w_kt, bf128, tile_in)
        out_a = _fox(qa, bq, kta, bkt, vaa, tile_s)

        lambda_init = 0.8 - 0.6 * math.exp(-0.3 * layer)
        lam_rows = jnp.zeros((4, LANES), F32)
        for r, p in enumerate((lambda_q1, lambda_k1, lambda_q2, lambda_k2)):
            lam_rows = lam_rows.at[r, :DIFF_HALF_DIM].set(p[layer].astype(F32))
        out_b = _diff(qb, posq, ktb, poskt, vb, slope_tab, lam_rows, subln_g[layer][None, :].astype(F32),
                      tile_s, lambda_init)

        keys = peer_sub_keys[layer].reshape(2 * PEER_HEADS, PEER_N_KEYS, PEER_HALF_DIM).astype(BF16)
        h2, hnt, e1, e2, kap = _route(
            out_a.reshape(t, FOX_WIDTH), out_b.reshape(t, DIFF_WIDTH), h.reshape(t, d),
            w_out[layer].astype(BF16), norm2_g[layer][None, :], peer_w_q[layer].astype(BF16), keys,
            tile_route)

        last = layer == depth - 1
        h = _peer(peer_u[layer].astype(BF16), hnt, peer_v[layer].T.astype(BF16), e1, e2, kap,
                  h2, final_g[None, :], tile_peer, rows_peer, last).reshape(b, s, d)
    return h
```

```python
import functools
import math

import jax
import jax.numpy as jnp
from jax import lax
from jax.experimental import pallas as pl
from jax.experimental.pallas import tpu as pltpu

F32 = jnp.float32
BF16 = jnp.bfloat16

D_MODEL = 1024
FOX_HEADS = 8
FOX_HEAD_DIM = 64
FOX_WIDTH = FOX_HEADS * FOX_HEAD_DIM
DIFF_HEADS = 4
DIFF_HALF_DIM = 64
DIFF_HEAD_DIM = 2 * DIFF_HALF_DIM
DIFF_WIDTH = DIFF_HEADS * DIFF_HEAD_DIM
CHUNK = 64
PEER_HEADS = 8
PEER_N_KEYS = 128
PEER_N_EXPERTS = PEER_N_KEYS * PEER_N_KEYS
PEER_HALF_DIM = 128
PEER_TOPK = 16
RMS_EPS = 1e-6

LANES = 128
SUBLANES = 8
NEG_BIG = -1e30
VMEM_LIMIT = 56 * 1024 * 1024

GATE_GROUPS = (0, 8, 16, 32, 40, 48)

NT_DIMS = (((1,), (1,)), ((), ()))


def _split3(v):
    hi = v.astype(BF16)
    r1 = v - hi.astype(F32)
    mid = r1.astype(BF16)
    lo = (r1 - mid.astype(F32)).astype(BF16)
    return hi, mid, lo


def _inproj_kernel(x_ref, g_ref, w_ref, wkt_ref, bf_ref, tril_ref,
                   qa_ref, bq_ref, kta_ref, bkt_ref, va_ref, qb_ref, ktb_ref, vb_ref, carry_ref):
    tm = x_ref.shape[0]

    @pl.when(pl.program_id(1) == 0)
    def _():
        carry_ref[...] = jnp.zeros_like(carry_ref)

    x = x_ref[...]
    ms = jnp.mean(x * x, axis=-1, keepdims=True)
    xb = (x * lax.rsqrt(ms + RMS_EPS) * g_ref[...]).astype(BF16)
    proj = jnp.dot(xb, w_ref[...], preferred_element_type=F32)
    kt = lax.dot_general(wkt_ref[...], xb, NT_DIMS, preferred_element_type=F32)

    o_va = FOX_WIDTH
    o_qb = o_va + 2 * FOX_WIDTH
    o_vb = o_qb + DIFF_WIDTH
    o_gate = o_vb + DIFF_WIDTH
    qa_ref[...] = (proj[:, 0:FOX_WIDTH] * 0.125).astype(BF16)
    qb_ref[...] = (proj[:, o_qb:o_qb + DIFF_WIDTH] * 0.125).astype(BF16)
    vb_ref[...] = proj[:, o_vb:o_vb + DIFF_WIDTH].astype(BF16)
    kta_ref[...] = kt[0:FOX_WIDTH, :].astype(BF16)
    ktb_ref[...] = kt[FOX_WIDTH:FOX_WIDTH + DIFF_WIDTH, :].astype(BF16)
    va = proj[:, o_va:o_qb]
    vlane = lax.broadcasted_iota(jnp.int32, va.shape, 1)
    va_ref[...] = jnp.where((vlane & FOX_HEAD_DIM) != 0, 1.0, va).astype(BF16)

    z = proj[:, o_gate:o_gate + LANES] + bf_ref[...]
    logf = jnp.minimum(z, 0.0) - jnp.log1p(jnp.exp(-jnp.abs(z)))
    tril = tril_ref[...]
    hi, mid, lo = _split3(logf)
    c = (jnp.dot(tril, hi, preferred_element_type=F32)
         + jnp.dot(tril, mid, preferred_element_type=F32)
         + jnp.dot(tril, lo, preferred_element_type=F32)) + carry_ref[0:1, :]
    carry_ref[...] = jnp.broadcast_to(c[tm - 1:tm, :], carry_ref.shape)

    chi, cmid, clo = _split3(c)
    chi, cmid, clo = chi.astype(F32), cmid.astype(F32), clo.astype(F32)
    lane = lax.broadcasted_iota(jnp.int32, c.shape, 1)
    ones_hi = (lane >= 32) & (lane < 56)
    bq = jnp.where(lane < 8, chi, jnp.where(lane < 16, cmid, jnp.where(lane < 24, clo,
                   jnp.where(ones_hi, 1.0, 0.0))))
    bk = jnp.where(lane < 24, 1.0, jnp.where(lane < 32, 0.0, jnp.where(lane < 40, -chi,
                   jnp.where(lane < 48, -cmid, jnp.where(lane < 56, -clo, 0.0)))))
    bq_ref[...] = bq.astype(BF16)
    bkt_ref[...] = bk.T.astype(BF16)


def _inproj(x, g, w_all, w_kt, bf128, tm):
    b, s, d = x.shape
    ncol = w_all.shape[1]
    nkt = w_kt.shape[0]
    tril = jnp.tril(jnp.ones((tm, tm), F32)).astype(BF16)
    row = lambda bi, si: (bi, si, 0)
    col = lambda bi, si: (bi, 0, si)
    const = lambda bi, si: (0, 0)
    shapes = [
        ((b, s, FOX_WIDTH), pl.BlockSpec((None, tm, FOX_WIDTH), row)),
        ((b, s, LANES), pl.BlockSpec((None, tm, LANES), row)),
        ((b, FOX_WIDTH, s), pl.BlockSpec((None, FOX_WIDTH, tm), col)),
        ((b, LANES, s), pl.BlockSpec((None, LANES, tm), col)),
        ((b, s, 2 * FOX_WIDTH), pl.BlockSpec((None, tm, 2 * FOX_WIDTH), row)),
        ((b, s, DIFF_WIDTH), pl.BlockSpec((None, tm, DIFF_WIDTH), row)),
        ((b, DIFF_WIDTH, s), pl.BlockSpec((None, DIFF_WIDTH, tm), col)),
        ((b, s, DIFF_WIDTH), pl.BlockSpec((None, tm, DIFF_WIDTH), row)),
    ]
    return pl.pallas_call(
        _inproj_kernel,
        grid=(b, s // tm),
        in_specs=[
            pl.BlockSpec((None, tm, d), row),
            pl.BlockSpec((1, d), const),
            pl.BlockSpec((d, ncol), const),
            pl.BlockSpec((nkt, d), const),
            pl.BlockSpec((1, LANES), const),
            pl.BlockSpec((tm, tm), const),
        ],
        out_specs=[spec for _, spec in shapes],
        out_shape=[jax.ShapeDtypeStruct(shape, BF16) for shape, _ in shapes],
        scratch_shapes=[pltpu.VMEM((SUBLANES, LANES), F32)],
        compiler_params=pltpu.CompilerParams(
            dimension_semantics=("arbitrary", "arbitrary"), vmem_limit_bytes=VMEM_LIMIT),
        name="inproj",
    )(x, g, w_all, w_kt, bf128, tril)


def _softmax_update(s, m, acc, v_aug):
    m_new = jnp.maximum(m, jnp.max(s, axis=-1, keepdims=True))
    p = jnp.exp(s - m_new).astype(BF16)
    acc_new = jnp.exp(m - m_new) * acc + jnp.dot(p, v_aug, preferred_element_type=F32)
    return m_new, acc_new


def _fox_kernel(q_ref, bq_ref, kt_ref, bkt_ref, v0_ref, v1_ref, o_ref):
    tq = q_ref.shape[0]
    pair = pl.program_id(1)
    qi = pl.program_id(2)
    lane = lax.broadcasted_iota(jnp.int32, (tq, LANES), 1)
    qf = q_ref[...].astype(F32)
    bqf = bq_ref[...].astype(F32)
    qa = []
    for hh in range(2):
        head = 2 * pair + hh
        qm = jnp.where((lane >= FOX_HEAD_DIM * hh) & (lane < FOX_HEAD_DIM * (hh + 1)), qf, 0.0)
        bm = jnp.where(((lane & 7) == head) & (lane < 56), bqf, 0.0)
        qa.append(jnp.concatenate([qm, bm], axis=1).astype(BF16))
    q_all = jnp.concatenate(qa, axis=0)
    v_refs = (v0_ref, v1_ref)

    def step(j, carry, causal):
        ks = pl.multiple_of(j * tq, tq)
        kt = jnp.concatenate([kt_ref[:, pl.ds(ks, tq)], bkt_ref[:, pl.ds(ks, tq)]], axis=0)
        s_all = jnp.dot(q_all, kt, preferred_element_type=F32)
        out = []
        for hh in range(2):
            s = s_all[hh * tq:(hh + 1) * tq]
            if causal:
                row = lax.broadcasted_iota(jnp.int32, s.shape, 0)
                col = lax.broadcasted_iota(jnp.int32, s.shape, 1)
                s = jnp.where(col <= row, s, -jnp.inf)
            m, acc = carry[hh]
            out.append(_softmax_update(s, m, acc, v_refs[hh][pl.ds(ks, tq), :]))
        return tuple(out)

    init = tuple((jnp.full((tq, 1), NEG_BIG, F32), jnp.zeros((tq, LANES), F32)) for _ in range(2))
    carry = lax.fori_loop(0, qi, lambda j, c: step(j, c, False), init)
    (_, acc0), (_, acc1) = step(qi, carry, True)
    o0 = acc0 / pltpu.roll(acc0, FOX_HEAD_DIM, 1)
    o1 = pltpu.roll(acc1, FOX_HEAD_DIM, 1) / acc1
    o_ref[...] = jnp.where(lane < FOX_HEAD_DIM, o0, o1).astype(BF16)


def _fox(qa, bq, kta, bkt, vaa, tq):
    b, s, _ = qa.shape
    npair = FOX_HEADS // 2
    return pl.pallas_call(
        _fox_kernel,
        grid=(b, npair, s // tq),
        in_specs=[
            pl.BlockSpec((None, tq, LANES), lambda bi, p, i: (bi, i, p)),
            pl.BlockSpec((None, tq, LANES), lambda bi, p, i: (bi, i, 0)),
            pl.BlockSpec((None, LANES, s), lambda bi, p, i: (bi, p, 0)),
            pl.BlockSpec((None, LANES, s), lambda bi, p, i: (bi, 0, 0)),
            pl.BlockSpec((None, s, LANES), lambda bi, p, i: (bi, 0, 2 * p)),
            pl.BlockSpec((None, s, LANES), lambda bi, p, i: (bi, 0, 2 * p + 1)),
        ],
        out_specs=pl.BlockSpec((None, tq, LANES), lambda bi, p, i: (bi, i, p)),
        out_shape=jax.ShapeDtypeStruct((b, s, FOX_WIDTH), BF16),
        compiler_params=pltpu.CompilerParams(
            dimension_semantics=("parallel", "parallel", "arbitrary"), vmem_limit_bytes=VMEM_LIMIT),
        name="fox",
    )(qa, bq, kta, bkt, vaa, vaa)


def _diff_kernel(q_ref, pq_ref, kt_ref, pkt_ref, v_ref, slope_ref, lam_ref, g_ref, o_ref, *, out_scale,
                 lambda_init):
    tq = q_ref.shape[0]
    qi = pl.program_id(2)
    lane = lax.broadcasted_iota(jnp.int32, (tq, LANES), 1)
    qf = q_ref[...].astype(F32)
    pq = pq_ref[...]
    q1 = jnp.where(lane < DIFF_HALF_DIM, qf, 0.0).astype(BF16)
    q2 = jnp.where(lane >= DIFF_HALF_DIM, qf, 0.0).astype(BF16)
    q_all = jnp.concatenate([jnp.concatenate([q1, pq], axis=1),
                             jnp.concatenate([q2, pq], axis=1)], axis=0)
    ones = jnp.ones((tq, LANES), BF16)

    def step(j, carry, diagonal):
        ks = pl.multiple_of(j * tq, tq)
        kt = jnp.concatenate([kt_ref[:, pl.ds(ks, tq)], pkt_ref[:, pl.ds(ks, tq)]], axis=0)
        s_all = jnp.dot(q_all, kt, preferred_element_type=F32)
        v_aug = jnp.concatenate([v_ref[pl.ds(ks, tq), :], ones], axis=1)
        if diagonal:
            row = lax.broadcasted_iota(jnp.int32, (tq, tq), 0)
            col = lax.broadcasted_iota(jnp.int32, (tq, tq), 1)
            fix = jnp.where((col // CHUNK) <= (row // CHUNK),
                            (-2.0 * slope_ref[0:1, 0:1]) * jnp.maximum(col - row, 0).astype(F32), -jnp.inf)
        out = []
        for t in range(2):
            s = s_all[t * tq:(t + 1) * tq]
            if diagonal:
                s = s + fix
            m, acc = carry[t]
            out.append(_softmax_update(s, m, acc, v_aug))
        return tuple(out)

    init = tuple((jnp.full((tq, 1), NEG_BIG, F32), jnp.zeros((tq, 2 * LANES), F32)) for _ in range(2))
    carry = lax.fori_loop(0, qi, lambda j, c: step(j, c, False), init)
    (_, acc1), (_, acc2) = step(qi, carry, True)

    lp = lam_ref[...]
    lam = (jnp.exp(jnp.sum(lp[0:1] * lp[1:2], axis=-1, keepdims=True))
           - jnp.exp(jnp.sum(lp[2:3] * lp[3:4], axis=-1, keepdims=True)) + lambda_init)
    o = acc1[:, :LANES] / acc1[:, LANES:] - lam * (acc2[:, :LANES] / acc2[:, LANES:])
    ms = jnp.mean(o * o, axis=-1, keepdims=True)
    o = o * lax.rsqrt(ms + RMS_EPS) * g_ref[...]
    o_ref[...] = (o * out_scale).astype(BF16)


def _diff(qb, posq, ktb, poskt, vb, slopes, lam_rows, subln_g, tq, lambda_init):
    b, s, _ = qb.shape
    kern = functools.partial(_diff_kernel, out_scale=1.0 - lambda_init, lambda_init=lambda_init)
    return pl.pallas_call(
        kern,
        grid=(b, DIFF_HEADS, s // tq),
        in_specs=[
            pl.BlockSpec((None, tq, LANES), lambda bi, h, i: (bi, i, h)),
            pl.BlockSpec((None, tq, LANES), lambda bi, h, i: (h, i, 0)),
            pl.BlockSpec((None, LANES, s), lambda bi, h, i: (bi, h, 0)),
            pl.BlockSpec((LANES, s), lambda bi, h, i: (0, 0)),
            pl.BlockSpec((None, s, LANES), lambda bi, h, i: (bi, 0, h)),
            pl.BlockSpec((None, SUBLANES, LANES), lambda bi, h, i: (h, 0, 0)),
            pl.BlockSpec((4, LANES), lambda bi, h, i: (0, 0)),
            pl.BlockSpec((1, LANES), lambda bi, h, i: (0, 0)),
        ],
        out_specs=pl.BlockSpec((None, tq, LANES), lambda bi, h, i: (bi, i, h)),
        out_shape=jax.ShapeDtypeStruct((b, s, DIFF_WIDTH), BF16),
        compiler_params=pltpu.CompilerParams(
            dimension_semantics=("parallel", "parallel", "arbitrary"), vmem_limit_bytes=VMEM_LIMIT),
        name="diff",
    )(qb, posq, ktb, poskt, vb, slopes, lam_rows, subln_g)


def _alibi_operands(s):
    pos = jnp.arange(s, dtype=jnp.int32)
    hi = (pos // 64).astype(F32)
    lo = (pos % 64).astype(F32)
    slopes = jnp.asarray([2.0 ** (-8.0 * (i + 1) / DIFF_HEADS) for i in range(DIFF_HEADS)], F32)
    zq = jnp.zeros((DIFF_HEADS, s, LANES), F32)
    sl = slopes[:, None]
    zq = zq.at[:, :, 0].set(-sl * 64.0 * hi[None]).at[:, :, 1].set(-sl * lo[None])
    zq = zq.at[:, :, 2].set(jnp.broadcast_to(sl * 64.0, (DIFF_HEADS, s)))
    zq = zq.at[:, :, 3].set(jnp.broadcast_to(sl, (DIFF_HEADS, s)))
    zk = jnp.zeros((LANES, s), F32)
    zk = zk.at[0].set(1.0).at[1].set(1.0).at[2].set(hi).at[3].set(lo)
    slope_tab = jnp.broadcast_to(slopes[:, None, None], (DIFF_HEADS, SUBLANES, LANES))
    return zq.astype(BF16), zk.astype(BF16), slope_tab


def _cmpx(xs, i, j):
    hi = jnp.maximum(xs[i], xs[j])
    lo = jnp.minimum(xs[i], xs[j])
    xs[i], xs[j] = hi, lo


def _bitonic_merge_desc(xs):
    n = len(xs)
    d = n // 2
    while d >= 1:
        for i in range(n):
            if (i & d) == 0:
                _cmpx(xs, i, i + d)
        d //= 2


def _sort_desc(xs):
    n = len(xs)
    k = 2
    while k <= n:
        d = k // 2
        while d >= 1:
            for i in range(n):
                l = i ^ d
                if l > i:
                    if (i & k) == 0:
                        _cmpx(xs, i, l)
                    else:
                        _cmpx(xs, l, i)
            d //= 2
        k *= 2


def _top16_desc(xs):
    xs = list(xs)
    _sort_desc(xs)
    for shift in (4, 2, 1):
        other = [pltpu.roll(xs[15 - i], shift, 0) for i in range(16)]
        xs = [jnp.maximum(xs[i], other[i]) for i in range(16)]
        _bitonic_merge_desc(xs)
    return xs


def _sublane_sum(x):
    for shift in (4, 2, 1):
        x = x + pltpu.roll(x, shift, 0)
    return x


def _route_kernel(oa_ref, ob_ref, x_ref, wo_ref, g_ref, wq_ref, keys_ref,
                  h_ref, hnt_ref, e1_ref, need_ref, e2_ref, rank_ref, sc_ref):
    tm = x_ref.shape[0]
    mixed = jnp.concatenate([oa_ref[...], ob_ref[...]], axis=1)
    h = x_ref[...] + jnp.dot(mixed, wo_ref[...], preferred_element_type=F32)
    h_ref[...] = h
    ms = jnp.mean(h * h, axis=-1, keepdims=True)
    hn_f = h * lax.rsqrt(ms + RMS_EPS) * g_ref[...]
    hn = hn_f.astype(BF16)
    hnt_ref[...] = hn_f.T.astype(BF16)
    q = jnp.dot(hn, wq_ref[...], preferred_element_type=F32).astype(BF16)
    for hp in range(2 * PEER_HEADS):
        qs = q[:, hp * PEER_HALF_DIM:(hp + 1) * PEER_HALF_DIM]
        sc_ref[hp] = lax.dot_general(keys_ref[hp], qs, NT_DIMS, preferred_element_type=F32)

    sub = lax.broadcasted_iota(jnp.int32, (SUBLANES, tm), 0)
    neg_inf = jnp.full((SUBLANES, tm), -jnp.inf, F32)
    groups = PEER_N_KEYS // SUBLANES

    def spread(vals):
        out = vals[SUBLANES - 1]
        for r in range(SUBLANES - 2, -1, -1):
            out = jnp.where(sub == r, vals[r], out)
        return out

    def head_body(hd, _):
        s1 = [sc_ref[2 * hd, g * SUBLANES:(g + 1) * SUBLANES, :] for g in range(groups)]
        s2 = [sc_ref[2 * hd + 1, g * SUBLANES:(g + 1) * SUBLANES, :] for g in range(groups)]
        a = _top16_desc(s1)
        b = _top16_desc(s2)
        b_lo, b_hi, a_hi = spread(b[:8]), spread(b[8:]), spread(a[8:])
        cands = [a[0] + b_lo, a[0] + b_hi, a[1] + b_lo]
        for i, n in ((2, 5), (3, 4), (4, 3), (5, 2), (6, 2), (7, 2)):
            cands.append(jnp.where(sub < n, a[i] + b_lo, neg_inf))
        cands.append(a_hi + b[0])
        top = _top16_desc(cands + [neg_inf] * (16 - len(cands)))
        tau = top[PEER_TOPK - 1]
        z = jnp.zeros((SUBLANES, tm), F32)
        for c in cands:
            z = z + jnp.where(c >= tau, jnp.exp(c - top[0]), 0.0)
        inv_z = 0.5 / _sublane_sum(z)
        for g0 in range(0, groups, 2):
            needs, ranks, e1s, e2s = [], [], [], []
            for g in (g0, g0 + 1):
                need = jnp.full((SUBLANES, tm), PEER_TOPK + 1.0, F32)
                rank = jnp.zeros((SUBLANES, tm), F32)
                for k in range(PEER_TOPK):
                    need = need - jnp.where(s1[g] + b[k] >= tau, 1.0, 0.0)
                    rank = rank + jnp.where(s2[g] >= b[k], 1.0, 0.0)
                needs.append(need)
                ranks.append(rank)
                e1s.append(jnp.exp(s1[g] - a[0]))
                e2s.append(jnp.exp(s2[g] - b[0]) * inv_z)
            rows = pl.ds(g0 * SUBLANES, 2 * SUBLANES)
            e1_ref[hd, rows, :] = jnp.concatenate(e1s, axis=0)
            need_ref[hd, rows, :] = jnp.concatenate(needs, axis=0)
            e2_ref[hd, rows, :] = jnp.concatenate(e2s, axis=0).astype(BF16)
            rank_ref[hd, rows, :] = jnp.concatenate(ranks, axis=0).astype(BF16)
        return 0

    lax.fori_loop(0, PEER_HEADS, head_body, 0)


def _route(oa, ob, x2, wo, g2, wq, keys, tm):
    t, d = x2.shape
    nq = wq.shape[1]
    row = lambda i: (i, 0)
    const2 = lambda i: (0, 0)
    tok3 = lambda i: (0, 0, i)
    score_spec = pl.BlockSpec((PEER_HEADS, PEER_N_KEYS, tm), tok3)
    score_shape = jax.ShapeDtypeStruct((PEER_HEADS, PEER_N_KEYS, t), F32)
    return pl.pallas_call(
        _route_kernel,
        grid=(t // tm,),
        in_specs=[
            pl.BlockSpec((tm, FOX_WIDTH), row),
            pl.BlockSpec((tm, DIFF_WIDTH), row),
            pl.BlockSpec((tm, d), row),
            pl.BlockSpec((d, d), const2),
            pl.BlockSpec((1, d), const2),
            pl.BlockSpec((d, nq), const2),
            pl.BlockSpec((2 * PEER_HEADS, PEER_N_KEYS, PEER_HALF_DIM), lambda i: (0, 0, 0)),
        ],
        out_specs=[
            pl.BlockSpec((tm, d), row),
            pl.BlockSpec((d, tm), lambda i: (0, i)),
            score_spec, score_spec, score_spec, score_spec,
        ],
        out_shape=[
            jax.ShapeDtypeStruct((t, d), F32),
            jax.ShapeDtypeStruct((d, t), BF16),
            score_shape, score_shape,
            jax.ShapeDtypeStruct(score_shape.shape, BF16),
            jax.ShapeDtypeStruct(score_shape.shape, BF16),
        ],
        scratch_shapes=[pltpu.VMEM((2 * PEER_HEADS, PEER_N_KEYS, tm), F32)],
        compiler_params=pltpu.CompilerParams(
            dimension_semantics=("parallel",), vmem_limit_bytes=VMEM_LIMIT),
        name="route",
    )(oa, ob, x2, wo, g2, wq, keys)


def _gelu2(x):
    return x * (1.0 + lax.erf(x * (1.0 / math.sqrt(2.0))))


MXU_TILE = 256
EW_ROWS = 16


def _peer_kernel(u0_ref, xt0_ref, un_ref, xtn_ref, vtp_ref, e1_ref, need_ref, e2_ref, rank_ref, h_ref, g_ref,
                 o_ref, acc_ref, hta_ref, htb_ref, wa_ref, wb_ref, row_ref, *, final_norm, tiles_per_block):
    n = pl.program_id(0)
    rows = un_ref.shape[0]
    tt = xtn_ref.shape[1]
    n_a = rows // PEER_N_KEYS
    col_tiles = tt // MXU_TILE
    assert (rows // MXU_TILE) * col_tiles == n_a

    @pl.when(n == 0)
    def _():
        acc_ref[...] = jnp.zeros_like(acc_ref)
        wb_ref[...] = jnp.zeros_like(wb_ref)
        hta_ref[...] = jnp.dot(u0_ref[...], xt0_ref[...], preferred_element_type=F32)

    def run(ht_cur, ht_nxt, w_cur, w_prev, stages):
        for al in stages:
            rt = pl.ds((al // col_tiles) * MXU_TILE, MXU_TILE)
            ct = pl.ds((al % col_tiles) * MXU_TILE, MXU_TILE)
            for hd in range(PEER_HEADS):
                row_ref[0, hd] = jnp.broadcast_to(e1_ref[hd, al:al + 1, :], (EW_ROWS, tt)).astype(BF16)
                row_ref[1, hd] = jnp.broadcast_to(need_ref[hd, al:al + 1, :], (EW_ROWS, tt)).astype(BF16)
            for rc in range(PEER_N_KEYS // EW_ROWS):
                rs = slice(rc * EW_ROWS, (rc + 1) * EW_ROWS)
                rd = pl.ds(al * PEER_N_KEYS + rc * EW_ROWS, EW_ROWS)
                act = _gelu2(ht_cur[rd, :]).astype(BF16)
                w = None
                for hd in range(PEER_HEADS):
                    gate = row_ref[0, hd] * e2_ref[hd, rs, :]
                    part = jnp.where(rank_ref[hd, rs, :] >= row_ref[1, hd], gate, jnp.zeros_like(gate))
                    w = part if w is None else w + part
                w_cur[rd, :] = act * w
            ht_nxt[rt, ct] = jnp.dot(un_ref[rt, :], xtn_ref[:, ct], preferred_element_type=F32)
            acc_ref[rt, ct] += jnp.dot(vtp_ref[rt, :], w_prev[:, ct], preferred_element_type=F32)

    @pl.when(n % 2 == 0)
    def _():
        run(hta_ref, htb_ref, wa_ref, wb_ref, range(n_a))

    @pl.when(n % 2 == 1)
    def _():
        run(htb_ref, hta_ref, wb_ref, wa_ref, range(n_a))

    @pl.when((n % tiles_per_block == 0) & (n > 0))
    def _():
        o = acc_ref[...].T + h_ref[...]
        if final_norm:
            ms = jnp.mean(o * o, axis=-1, keepdims=True)
            o = o * lax.rsqrt(ms + RMS_EPS) * g_ref[...]
        o_ref[...] = o
        acc_ref[...] = jnp.zeros_like(acc_ref)


def _peer(u, hnt, vt, e1, need, e2, rank, h, g, tt, rows, final_norm):
    d, t = hnt.shape
    ne = u.shape[0]
    n_a = rows // PEER_N_KEYS
    assert n_a % SUBLANES == 0 and d == rows
    per_block = ne // rows
    n_blocks = t // tt
    n_tiles = per_block * n_blocks
    tok = lambda n: jnp.minimum(n // per_block, n_blocks - 1)
    tok_prev = lambda n: jnp.maximum(n - 1, 0) // per_block
    kern = functools.partial(_peer_kernel, final_norm=final_norm, tiles_per_block=per_block)
    first = pl.BlockSpec((PEER_HEADS, n_a, tt), lambda n: (0, n % per_block, tok(n)))
    second = pl.BlockSpec((PEER_HEADS, PEER_N_KEYS, tt), lambda n: (0, 0, tok(n)))
    return pl.pallas_call(
        kern,
        grid=(n_tiles + 1,),
        in_specs=[
            pl.BlockSpec((rows, d), lambda n: (0, 0)),
            pl.BlockSpec((d, tt), lambda n: (0, 0)),
            pl.BlockSpec((rows, d), lambda n: ((n + 1) % per_block, 0)),
            pl.BlockSpec((d, tt), lambda n: (0, tok(n + 1))),
            pl.BlockSpec((d, rows), lambda n: (0, (n + per_block - 1) % per_block)),
            first, first, second, second,
            pl.BlockSpec((tt, d), lambda n: (tok_prev(n), 0)),
            pl.BlockSpec((1, d), lambda n: (0, 0)),
        ],
        out_specs=pl.BlockSpec((tt, d), lambda n: (tok_prev(n), 0)),
        out_shape=jax.ShapeDtypeStruct((t, d), F32),
        scratch_shapes=[
            pltpu.VMEM((d, tt), F32),
            pltpu.VMEM((rows, tt), F32),
            pltpu.VMEM((rows, tt), F32),
            pltpu.VMEM((rows, tt), BF16),
            pltpu.VMEM((rows, tt), BF16),
            pltpu.VMEM((2, PEER_HEADS, EW_ROWS, tt), BF16),
        ],
        compiler_params=pltpu.CompilerParams(
            dimension_semantics=("arbitrary",), vmem_limit_bytes=VMEM_LIMIT),
        name="peer",
    )(u, hnt, u, hnt, vt, e1, need, e2, rank, h, g)


def _largest_tile(n, cap):
    t = cap
    while n % t:
        t //= 2
    return t


def kernel(x, norm1_g, w_in, b_f, lambda_q1, lambda_k1, lambda_q2, lambda_k2, subln_g, w_out,
           norm2_g, peer_w_q, peer_sub_keys, peer_u, peer_v, final_g):
    b, s, d = x.shape
    depth = w_in.shape[0]
    t = b * s
    tile_s = _largest_tile(s, 512)
    tile_in = _largest_tile(s, 512)
    tile_route = _largest_tile(t, 256)
    tile_peer = _largest_tile(t, 512)
    rows_peer = 1024

    posq, poskt, slope_tab = _alibi_operands(s)
    h = x
    for layer in range(depth):
        w = w_in[layer]
        o = 0
        parts = {}
        for name, width in (("qa", FOX_WIDTH), ("ka", FOX_WIDTH), ("va", FOX_WIDTH), ("f", FOX_HEADS),
                            ("qb", DIFF_WIDTH), ("kb", DIFF_WIDTH), ("vb", DIFF_WIDTH)):
            parts[name] = w[:, o:o + width]
            o += width
        gate = jnp.zeros((d, LANES), F32)
        bf128 = jnp.zeros((1, LANES), F32)
        for g0 in GATE_GROUPS:
            gate = gate.at[:, g0:g0 + FOX_HEADS].set(parts["f"])
            bf128 = bf128.at[0, g0:g0 + FOX_HEADS].set(b_f[layer])
        va_pad = jnp.pad(parts["va"].reshape(d, FOX_HEADS, FOX_HEAD_DIM),
                         ((0, 0), (0, 0), (0, FOX_HEAD_DIM))).reshape(d, 2 * FOX_WIDTH)
        w_all = jnp.concatenate([parts["qa"], va_pad, parts["qb"], parts["vb"], gate], axis=1).astype(BF16)
        w_kt = jnp.concatenate([parts["ka"], parts["kb"]], axis=1).T.astype(BF16)

        qa, bq, kta, bkt, vaa, qb, ktb, vb = _inproj(h, norm1_g[layer][None, :], w_all, w_kt, bf128, tile_in)
        out_a = _fox(qa, bq, kta, bkt, vaa, tile_s)

        lambda_init = 0.8 - 0.6 * math.exp(-0.3 * layer)
        lam_rows = jnp.zeros((4, LANES), F32)
        for r, p in enumerate((lambda_q1, lambda_k1, lambda_q2, lambda_k2)):
            lam_rows = lam_rows.at[r, :DIFF_HALF_DIM].set(p[layer].astype(F32))
        out_b = _diff(qb, posq, ktb, poskt, vb, slope_tab, lam_rows, subln_g[layer][None, :].astype(F32),
                      tile_s, lambda_init)

        keys = peer_sub_keys[layer].reshape(2 * PEER_HEADS, PEER_N_KEYS, PEER_HALF_DIM).astype(BF16)
        h2, hnt, e1, need, e2, rank = _route(
            out_a.reshape(t, FOX_WIDTH), out_b.reshape(t, DIFF_WIDTH), h.reshape(t, d),
            w_out[layer].astype(BF16), norm2_g[layer][None, :], peer_w_q[layer].astype(BF16), keys,
            tile_route)

        last = layer == depth - 1
        h = _peer(peer_u[layer].astype(BF16), hnt, peer_v[layer].T.astype(BF16), e1, need, e2, rank,
                  h2, final_g[None, :], tile_peer, rows_peer, last).reshape(b, s, d)
    return h
```

```python
import functools
import math

import jax
import jax.numpy as jnp
from jax import lax
from jax.experimental import pallas as pl
from jax.experimental.pallas import tpu as pltpu

F32 = jnp.float32
BF16 = jnp.bfloat16

D_MODEL = 1024
FOX_HEADS = 8
FOX_HEAD_DIM = 64
FOX_WIDTH = FOX_HEADS * FOX_HEAD_DIM
DIFF_HEADS = 4
DIFF_HALF_DIM = 64
DIFF_HEAD_DIM = 2 * DIFF_HALF_DIM
DIFF_WIDTH = DIFF_HEADS * DIFF_HEAD_DIM
CHUNK = 64
PEER_HEADS = 8
PEER_N_KEYS = 128
PEER_N_EXPERTS = PEER_N_KEYS * PEER_N_KEYS
PEER_HALF_DIM = 128
PEER_TOPK = 16
RMS_EPS = 1e-6

LANES = 128
SUBLANES = 8
NEG_BIG = -1e30
VMEM_LIMIT = 56 * 1024 * 1024

GATE_GROUPS = (0, 8, 16, 32, 40, 48)

NT_DIMS = (((1,), (1,)), ((), ()))


def _split3(v):
    hi = v.astype(BF16)
    r1 = v - hi.astype(F32)
    mid = r1.astype(BF16)
    lo = (r1 - mid.astype(F32)).astype(BF16)
    return hi, mid, lo


def _inproj_kernel(x_ref, g_ref, w_ref, wkt_ref, bf_ref, tril_ref,
                   qa_ref, bq_ref, kta_ref, bkt_ref, va_ref, qb_ref, ktb_ref, vb_ref, carry_ref):
    tm = x_ref.shape[0]

    @pl.when(pl.program_id(1) == 0)
    def _():
        carry_ref[...] = jnp.zeros_like(carry_ref)

    x = x_ref[...]
    ms = jnp.mean(x * x, axis=-1, keepdims=True)
    xb = (x * lax.rsqrt(ms + RMS_EPS) * g_ref[...]).astype(BF16)
    proj = jnp.dot(xb, w_ref[...], preferred_element_type=F32)
    kt = lax.dot_general(wkt_ref[...], xb, NT_DIMS, preferred_element_type=F32)

    o_va = FOX_WIDTH
    o_qb = o_va + 2 * FOX_WIDTH
    o_vb = o_qb + DIFF_WIDTH
    o_gate = o_vb + DIFF_WIDTH
    qa_ref[...] = (proj[:, 0:FOX_WIDTH] * 0.125).astype(BF16)
    qb_ref[...] = (proj[:, o_qb:o_qb + DIFF_WIDTH] * 0.125).astype(BF16)
    vb_ref[...] = proj[:, o_vb:o_vb + DIFF_WIDTH].astype(BF16)
    kta_ref[...] = kt[0:FOX_WIDTH, :].astype(BF16)
    ktb_ref[...] = kt[FOX_WIDTH:FOX_WIDTH + DIFF_WIDTH, :].astype(BF16)
    va = proj[:, o_va:o_qb]
    vlane = lax.broadcasted_iota(jnp.int32, va.shape, 1)
    va_ref[...] = jnp.where((vlane & FOX_HEAD_DIM) != 0, 1.0, va).astype(BF16)

    z = proj[:, o_gate:o_gate + LANES] + bf_ref[...]
    logf = jnp.minimum(z, 0.0) - jnp.log1p(jnp.exp(-jnp.abs(z)))
    tril = tril_ref[...]
    hi, mid, lo = _split3(logf)
    c = (jnp.dot(tril, hi, preferred_element_type=F32)
         + jnp.dot(tril, mid, preferred_element_type=F32)
         + jnp.dot(tril, lo, preferred_element_type=F32)) + carry_ref[0:1, :]
    carry_ref[...] = jnp.broadcast_to(c[tm - 1:tm, :], carry_ref.shape)

    chi, cmid, clo = _split3(c)
    chi, cmid, clo = chi.astype(F32), cmid.astype(F32), clo.astype(F32)
    lane = lax.broadcasted_iota(jnp.int32, c.shape, 1)
    ones_hi = (lane >= 32) & (lane < 56)
    bq = jnp.where(lane < 8, chi, jnp.where(lane < 16, cmid, jnp.where(lane < 24, clo,
                   jnp.where(ones_hi, 1.0, 0.0))))
    bk = jnp.where(lane < 24, 1.0, jnp.where(lane < 32, 0.0, jnp.where(lane < 40, -chi,
                   jnp.where(lane < 48, -cmid, jnp.where(lane < 56, -clo, 0.0)))))
    bq_ref[...] = bq.astype(BF16)
    bkt_ref[...] = bk.T.astype(BF16)


def _inproj(x, g, w_all, w_kt, bf128, tm):
    b, s, d = x.shape
    ncol = w_all.shape[1]
    nkt = w_kt.shape[0]
    tril = jnp.tril(jnp.ones((tm, tm), F32)).astype(BF16)
    row = lambda bi, si: (bi, si, 0)
    col = lambda bi, si: (bi, 0, si)
    const = lambda bi, si: (0, 0)
    shapes = [
        ((b, s, FOX_WIDTH), pl.BlockSpec((None, tm, FOX_WIDTH), row)),
        ((b, s, LANES), pl.BlockSpec((None, tm, LANES), row)),
        ((b, FOX_WIDTH, s), pl.BlockSpec((None, FOX_WIDTH, tm), col)),
        ((b, LANES, s), pl.BlockSpec((None, LANES, tm), col)),
        ((b, s, 2 * FOX_WIDTH), pl.BlockSpec((None, tm, 2 * FOX_WIDTH), row)),
        ((b, s, DIFF_WIDTH), pl.BlockSpec((None, tm, DIFF_WIDTH), row)),
        ((b, DIFF_WIDTH, s), pl.BlockSpec((None, DIFF_WIDTH, tm), col)),
        ((b, s, DIFF_WIDTH), pl.BlockSpec((None, tm, DIFF_WIDTH), row)),
    ]
    return pl.pallas_call(
        _inproj_kernel,
        grid=(b, s // tm),
        in_specs=[
            pl.BlockSpec((None, tm, d), row),
            pl.BlockSpec((1, d), const),
            pl.BlockSpec((d, ncol), const),
            pl.BlockSpec((nkt, d), const),
            pl.BlockSpec((1, LANES), const),
            pl.BlockSpec((tm, tm), const),
        ],
        out_specs=[spec for _, spec in shapes],
        out_shape=[jax.ShapeDtypeStruct(shape, BF16) for shape, _ in shapes],
        scratch_shapes=[pltpu.VMEM((SUBLANES, LANES), F32)],
        compiler_params=pltpu.CompilerParams(
            dimension_semantics=("arbitrary", "arbitrary"), vmem_limit_bytes=VMEM_LIMIT),
        name="inproj",
    )(x, g, w_all, w_kt, bf128, tril)


def _softmax_update(s, m, acc, v_aug):
    m_new = jnp.maximum(m, jnp.max(s, axis=-1, keepdims=True))
    p = jnp.exp(s - m_new).astype(BF16)
    acc_new = jnp.exp(m - m_new) * acc + jnp.dot(p, v_aug, preferred_element_type=F32)
    return m_new, acc_new


def _fox_kernel(q_ref, bq_ref, kt_ref, bkt_ref, v0_ref, v1_ref, o_ref):
    tq = q_ref.shape[0]
    pair = pl.program_id(1)
    qi = pl.program_id(2)
    lane = lax.broadcasted_iota(jnp.int32, (tq, LANES), 1)
    qf = q_ref[...].astype(F32)
    bqf = bq_ref[...].astype(F32)
    qa = []
    for hh in range(2):
        head = 2 * pair + hh
        qm = jnp.where((lane >= FOX_HEAD_DIM * hh) & (lane < FOX_HEAD_DIM * (hh + 1)), qf, 0.0)
        bm = jnp.where(((lane & 7) == head) & (lane < 56), bqf, 0.0)
        qa.append(jnp.concatenate([qm, bm], axis=1).astype(BF16))
    q_all = jnp.concatenate(qa, axis=0)
    v_refs = (v0_ref, v1_ref)

    def step(j, carry, causal):
        ks = pl.multiple_of(j * tq, tq)
        kt = jnp.concatenate([kt_ref[:, pl.ds(ks, tq)], bkt_ref[:, pl.ds(ks, tq)]], axis=0)
        s_all = jnp.dot(q_all, kt, preferred_element_type=F32)
        out = []
        for hh in range(2):
            s = s_all[hh * tq:(hh + 1) * tq]
            if causal:
                row = lax.broadcasted_iota(jnp.int32, s.shape, 0)
                col = lax.broadcasted_iota(jnp.int32, s.shape, 1)
                s = jnp.where(col <= row, s, -jnp.inf)
            m, acc = carry[hh]
            out.append(_softmax_update(s, m, acc, v_refs[hh][pl.ds(ks, tq), :]))
        return tuple(out)

    init = tuple((jnp.full((tq, 1), NEG_BIG, F32), jnp.zeros((tq, LANES), F32)) for _ in range(2))
    carry = lax.fori_loop(0, qi, lambda j, c: step(j, c, False), init)
    (_, acc0), (_, acc1) = step(qi, carry, True)
    o0 = acc0 / pltpu.roll(acc0, FOX_HEAD_DIM, 1)
    o1 = pltpu.roll(acc1, FOX_HEAD_DIM, 1) / acc1
    o_ref[...] = jnp.where(lane < FOX_HEAD_DIM, o0, o1).astype(BF16)


def _fox(qa, bq, kta, bkt, vaa, tq):
    b, s, _ = qa.shape
    npair = FOX_HEADS // 2
    return pl.pallas_call(
        _fox_kernel,
        grid=(b, npair, s // tq),
        in_specs=[
            pl.BlockSpec((None, tq, LANES), lambda bi, p, i: (bi, i, p)),
            pl.BlockSpec((None, tq, LANES), lambda bi, p, i: (bi, i, 0)),
            pl.BlockSpec((None, LANES, s), lambda bi, p, i: (bi, p, 0)),
            pl.BlockSpec((None, LANES, s), lambda bi, p, i: (bi, 0, 0)),
            pl.BlockSpec((None, s, LANES), lambda bi, p, i: (bi, 0, 2 * p)),
            pl.BlockSpec((None, s, LANES), lambda bi, p, i: (bi, 0, 2 * p + 1)),
        ],
        out_specs=pl.BlockSpec((None, tq, LANES), lambda bi, p, i: (bi, i, p)),
        out_shape=jax.ShapeDtypeStruct((b, s, FOX_WIDTH), BF16),
        compiler_params=pltpu.CompilerParams(
            dimension_semantics=("parallel", "parallel", "arbitrary"), vmem_limit_bytes=VMEM_LIMIT),
        name="fox",
    )(qa, bq, kta, bkt, vaa, vaa)


def _diff_kernel(q_ref, pq_ref, kt_ref, pkt_ref, v_ref, slope_ref, lam_ref, g_ref, o_ref, *, out_scale,
                 lambda_init):
    tq = q_ref.shape[0]
    qi = pl.program_id(2)
    lane = lax.broadcasted_iota(jnp.int32, (tq, LANES), 1)
    qf = q_ref[...].astype(F32)
    pq = pq_ref[...]
    q1 = jnp.where(lane < DIFF_HALF_DIM, qf, 0.0).astype(BF16)
    q2 = jnp.where(lane >= DIFF_HALF_DIM, qf, 0.0).astype(BF16)
    q_all = jnp.concatenate([jnp.concatenate([q1, pq], axis=1),
                             jnp.concatenate([q2, pq], axis=1)], axis=0)
    ones = jnp.ones((tq, LANES), BF16)

    def step(j, carry, diagonal):
        ks = pl.multiple_of(j * tq, tq)
        kt = jnp.concatenate([kt_ref[:, pl.ds(ks, tq)], pkt_ref[:, pl.ds(ks, tq)]], axis=0)
        s_all = jnp.dot(q_all, kt, preferred_element_type=F32)
        v_aug = jnp.concatenate([v_ref[pl.ds(ks, tq), :], ones], axis=1)
        if diagonal:
            row = lax.broadcasted_iota(jnp.int32, (tq, tq), 0)
            col = lax.broadcasted_iota(jnp.int32, (tq, tq), 1)
            fix = jnp.where((col // CHUNK) <= (row // CHUNK),
                            (-2.0 * slope_ref[0:1, 0:1]) * jnp.maximum(col - row, 0).astype(F32), -jnp.inf)
        out = []
        for t in range(2):
            s = s_all[t * tq:(t + 1) * tq]
            if diagonal:
                s = s + fix
            m, acc = carry[t]
            out.append(_softmax_update(s, m, acc, v_aug))
        return tuple(out)

    init = tuple((jnp.full((tq, 1), NEG_BIG, F32), jnp.zeros((tq, 2 * LANES), F32)) for _ in range(2))
    carry = lax.fori_loop(0, qi, lambda j, c: step(j, c, False), init)
    (_, acc1), (_, acc2) = step(qi, carry, True)

    lp = lam_ref[...]
    lam = (jnp.exp(jnp.sum(lp[0:1] * lp[1:2], axis=-1, keepdims=True))
           - jnp.exp(jnp.sum(lp[2:3] * lp[3:4], axis=-1, keepdims=True)) + lambda_init)
    o = acc1[:, :LANES] / acc1[:, LANES:] - lam * (acc2[:, :LANES] / acc2[:, LANES:])
    ms = jnp.mean(o * o, axis=-1, keepdims=True)
    o = o * lax.rsqrt(ms + RMS_EPS) * g_ref[...]
    o_ref[...] = (o * out_scale).astype(BF16)


def _diff(qb, posq, ktb, poskt, vb, slopes, lam_rows, subln_g, tq, lambda_init):
    b, s, _ = qb.shape
    kern = functools.partial(_diff_kernel, out_scale=1.0 - lambda_init, lambda_init=lambda_init)
    return pl.pallas_call(
        kern,
        grid=(b, DIFF_HEADS, s // tq),
        in_specs=[
            pl.BlockSpec((None, tq, LANES), lambda bi, h, i: (bi, i, h)),
            pl.BlockSpec((None, tq, LANES), lambda bi, h, i: (h, i, 0)),
            pl.BlockSpec((None, LANES, s), lambda bi, h, i: (bi, h, 0)),
            pl.BlockSpec((LANES, s), lambda bi, h, i: (0, 0)),
            pl.BlockSpec((None, s, LANES), lambda bi, h, i: (bi, 0, h)),
            pl.BlockSpec((None, SUBLANES, LANES), lambda bi, h, i: (h, 0, 0)),
            pl.BlockSpec((4, LANES), lambda bi, h, i: (0, 0)),
            pl.BlockSpec((1, LANES), lambda bi, h, i: (0, 0)),
        ],
        out_specs=pl.BlockSpec((None, tq, LANES), lambda bi, h, i: (bi, i, h)),
        out_shape=jax.ShapeDtypeStruct((b, s, DIFF_WIDTH), BF16),
        compiler_params=pltpu.CompilerParams(
            dimension_semantics=("parallel", "parallel", "arbitrary"), vmem_limit_bytes=VMEM_LIMIT),
        name="diff",
    )(qb, posq, ktb, poskt, vb, slopes, lam_rows, subln_g)


def _alibi_operands(s):
    pos = jnp.arange(s, dtype=jnp.int32)
    hi = (pos // 64).astype(F32)
    lo = (pos % 64).astype(F32)
    slopes = jnp.asarray([2.0 ** (-8.0 * (i + 1) / DIFF_HEADS) for i in range(DIFF_HEADS)], F32)
    zq = jnp.zeros((DIFF_HEADS, s, LANES), F32)
    sl = slopes[:, None]
    zq = zq.at[:, :, 0].set(-sl * 64.0 * hi[None]).at[:, :, 1].set(-sl * lo[None])
    zq = zq.at[:, :, 2].set(jnp.broadcast_to(sl * 64.0, (DIFF_HEADS, s)))
    zq = zq.at[:, :, 3].set(jnp.broadcast_to(sl, (DIFF_HEADS, s)))
    zk = jnp.zeros((LANES, s), F32)
    zk = zk.at[0].set(1.0).at[1].set(1.0).at[2].set(hi).at[3].set(lo)
    slope_tab = jnp.broadcast_to(slopes[:, None, None], (DIFF_HEADS, SUBLANES, LANES))
    return zq.astype(BF16), zk.astype(BF16), slope_tab


def _cmpx(xs, i, j):
    hi = jnp.maximum(xs[i], xs[j])
    lo = jnp.minimum(xs[i], xs[j])
    xs[i], xs[j] = hi, lo


def _bitonic_merge_desc(xs):
    n = len(xs)
    d = n // 2
    while d >= 1:
        for i in range(n):
            if (i & d) == 0:
                _cmpx(xs, i, i + d)
        d //= 2


def _sort_desc(xs):
    n = len(xs)
    k = 2
    while k <= n:
        d = k // 2
        while d >= 1:
            for i in range(n):
                l = i ^ d
                if l > i:
                    if (i & k) == 0:
                        _cmpx(xs, i, l)
                    else:
                        _cmpx(xs, l, i)
            d //= 2
        k *= 2


def _top16_desc(xs):
    xs = list(xs)
    _sort_desc(xs)
    for shift in (4, 2, 1):
        other = [pltpu.roll(xs[15 - i], shift, 0) for i in range(16)]
        xs = [jnp.maximum(xs[i], other[i]) for i in range(16)]
        _bitonic_merge_desc(xs)
    return xs


def _sublane_sum(x):
    for shift in (4, 2, 1):
        x = x + pltpu.roll(x, shift, 0)
    return x


def _route_kernel(oa_ref, ob_ref, x_ref, wo_ref, g_ref, wq_ref, keys_ref,
                  h_ref, hnt_ref, e1_ref, need_ref, e2_ref, rank_ref, sc_ref):
    tm = x_ref.shape[0]
    mixed = jnp.concatenate([oa_ref[...], ob_ref[...]], axis=1)
    h = x_ref[...] + jnp.dot(mixed, wo_ref[...], preferred_element_type=F32)
    h_ref[...] = h
    ms = jnp.mean(h * h, axis=-1, keepdims=True)
    hn_f = h * lax.rsqrt(ms + RMS_EPS) * g_ref[...]
    hn = hn_f.astype(BF16)
    hnt_ref[...] = hn_f.T.astype(BF16)
    q = jnp.dot(hn, wq_ref[...], preferred_element_type=F32).astype(BF16)
    for hp in range(2 * PEER_HEADS):
        qs = q[:, hp * PEER_HALF_DIM:(hp + 1) * PEER_HALF_DIM]
        sc_ref[hp] = lax.dot_general(keys_ref[hp], qs, NT_DIMS, preferred_element_type=F32)

    sub = lax.broadcasted_iota(jnp.int32, (SUBLANES, tm), 0)
    neg_inf = jnp.full((SUBLANES, tm), -jnp.inf, F32)
    groups = PEER_N_KEYS // SUBLANES

    def spread(vals):
        out = vals[SUBLANES - 1]
        for r in range(SUBLANES - 2, -1, -1):
            out = jnp.where(sub == r, vals[r], out)
        return out

    def head_body(hd, _):
        s1 = [sc_ref[2 * hd, g * SUBLANES:(g + 1) * SUBLANES, :] for g in range(groups)]
        s2 = [sc_ref[2 * hd + 1, g * SUBLANES:(g + 1) * SUBLANES, :] for g in range(groups)]
        a = _top16_desc(s1)
        b = _top16_desc(s2)
        b_lo, b_hi, a_hi = spread(b[:8]), spread(b[8:]), spread(a[8:])
        cands = [a[0] + b_lo, a[0] + b_hi, a[1] + b_lo]
        for i, n in ((2, 5), (3, 4), (4, 3), (5, 2), (6, 2), (7, 2)):
            cands.append(jnp.where(sub < n, a[i] + b_lo, neg_inf))
        cands.append(a_hi + b[0])
        top = _top16_desc(cands + [neg_inf] * (16 - len(cands)))
        tau = top[PEER_TOPK - 1]
        z = jnp.zeros((SUBLANES, tm), F32)
        for c in cands:
            z = z + jnp.where(c >= tau, jnp.exp(c - top[0]), 0.0)
        inv_z = 0.5 / _sublane_sum(z)
        for g0 in range(0, groups, 2):
            needs, ranks, e1s, e2s = [], [], [], []
            for g in (g0, g0 + 1):
                need = jnp.full((SUBLANES, tm), PEER_TOPK + 1.0, F32)
                rank = jnp.zeros((SUBLANES, tm), F32)
                for k in range(PEER_TOPK):
                    need = jnp.where(s1[g] + b[k] >= tau, float(PEER_TOPK - k), need)
                for k in reversed(range(PEER_TOPK)):
                    rank = jnp.where(s2[g] >= b[k], float(PEER_TOPK - k), rank)
                needs.append(need)
                ranks.append(rank)
                e1s.append(jnp.exp(s1[g] - a[0]))
                e2s.append(jnp.exp(s2[g] - b[0]) * inv_z)
            rows = pl.ds(g0 * SUBLANES, 2 * SUBLANES)
            e1_ref[hd, rows, :] = jnp.concatenate(e1s, axis=0)
            need_ref[hd, rows, :] = jnp.concatenate(needs, axis=0)
            e2_ref[hd, rows, :] = jnp.concatenate(e2s, axis=0).astype(BF16)
            rank_ref[hd, rows, :] = jnp.concatenate(ranks, axis=0).astype(BF16)
        return 0

    lax.fori_loop(0, PEER_HEADS, head_body, 0)


def _route(oa, ob, x2, wo, g2, wq, keys, tm):
    t, d = x2.shape
    nq = wq.shape[1]
    row = lambda i: (i, 0)
    const2 = lambda i: (0, 0)
    tok3 = lambda i: (0, 0, i)
    score_spec = pl.BlockSpec((PEER_HEADS, PEER_N_KEYS, tm), tok3)
    score_shape = jax.ShapeDtypeStruct((PEER_HEADS, PEER_N_KEYS, t), F32)
    return pl.pallas_call(
        _route_kernel,
        grid=(t // tm,),
        in_specs=[
            pl.BlockSpec((tm, FOX_WIDTH), row),
            pl.BlockSpec((tm, DIFF_WIDTH), row),
            pl.BlockSpec((tm, d), row),
            pl.BlockSpec((d, d), const2),
            pl.BlockSpec((1, d), const2),
            pl.BlockSpec((d, nq), const2),
            pl.BlockSpec((2 * PEER_HEADS, PEER_N_KEYS, PEER_HALF_DIM), lambda i: (0, 0, 0)),
        ],
        out_specs=[
            pl.BlockSpec((tm, d), row),
            pl.BlockSpec((d, tm), lambda i: (0, i)),
            score_spec, score_spec, score_spec, score_spec,
        ],
        out_shape=[
            jax.ShapeDtypeStruct((t, d), F32),
            jax.ShapeDtypeStruct((d, t), BF16),
            score_shape, score_shape,
            jax.ShapeDtypeStruct(score_shape.shape, BF16),
            jax.ShapeDtypeStruct(score_shape.shape, BF16),
        ],
        scratch_shapes=[pltpu.VMEM((2 * PEER_HEADS, PEER_N_KEYS, tm), F32)],
        compiler_params=pltpu.CompilerParams(
            dimension_semantics=("parallel",), vmem_limit_bytes=VMEM_LIMIT),
        name="route",
    )(oa, ob, x2, wo, g2, wq, keys)


def _gelu2(x):
    return x * (1.0 + lax.erf(x * (1.0 / math.sqrt(2.0))))


MXU_TILE = 256
EW_ROWS = 16


def _peer_kernel(u0_ref, xt0_ref, un_ref, xtn_ref, vtp_ref, e1_ref, need_ref, e2_ref, rank_ref, h_ref, g_ref,
                 o_ref, acc_ref, hta_ref, htb_ref, wa_ref, wb_ref, row_ref, *, final_norm, tiles_per_block):
    n = pl.program_id(0)
    rows = un_ref.shape[0]
    tt = xtn_ref.shape[1]
    n_a = rows // PEER_N_KEYS
    col_tiles = tt // MXU_TILE
    assert (rows // MXU_TILE) * col_tiles == n_a

    @pl.when(n == 0)
    def _():
        acc_ref[...] = jnp.zeros_like(acc_ref)
        wb_ref[...] = jnp.zeros_like(wb_ref)
        hta_ref[...] = jnp.dot(u0_ref[...], xt0_ref[...], preferred_element_type=F32)

    def run(ht_cur, ht_nxt, w_cur, w_prev, stages):
        for al in stages:
            rt = pl.ds((al // col_tiles) * MXU_TILE, MXU_TILE)
            ct = pl.ds((al % col_tiles) * MXU_TILE, MXU_TILE)
            for hd in range(PEER_HEADS):
                row_ref[0, hd] = jnp.broadcast_to(e1_ref[hd, al:al + 1, :], (EW_ROWS, tt)).astype(BF16)
                row_ref[1, hd] = jnp.broadcast_to(need_ref[hd, al:al + 1, :], (EW_ROWS, tt)).astype(BF16)
            for rc in range(0, PEER_N_KEYS // EW_ROWS, 2):
                tiles = [slice((rc + i) * EW_ROWS, (rc + i + 1) * EW_ROWS) for i in range(2)]
                w = [None, None]
                for hd in range(PEER_HEADS):
                    e1t = row_ref[0, hd]
                    needt = row_ref[1, hd]
                    for i, rs in enumerate(tiles):
                        gate = e1t * e2_ref[hd, rs, :]
                        part = jnp.where(rank_ref[hd, rs, :] >= needt, gate, jnp.zeros_like(gate))
                        w[i] = part if w[i] is None else w[i] + part
                for i in range(2):
                    rd = pl.ds(al * PEER_N_KEYS + (rc + i) * EW_ROWS, EW_ROWS)
                    w_cur[rd, :] = _gelu2(ht_cur[rd, :]).astype(BF16) * w[i]
            ht_nxt[rt, ct] = jnp.dot(un_ref[rt, :], xtn_ref[:, ct], preferred_element_type=F32)
            acc_ref[rt, ct] += jnp.dot(vtp_ref[rt, :], w_prev[:, ct], preferred_element_type=F32)

    @pl.when(n % 2 == 0)
    def _():
        run(hta_ref, htb_ref, wa_ref, wb_ref, range(n_a))

    @pl.when(n % 2 == 1)
    def _():
        run(htb_ref, hta_ref, wb_ref, wa_ref, range(n_a))

    @pl.when((n % tiles_per_block == 0) & (n > 0))
    def _():
        o = acc_ref[...].T + h_ref[...]
        if final_norm:
            ms = jnp.mean(o * o, axis=-1, keepdims=True)
            o = o * lax.rsqrt(ms + RMS_EPS) * g_ref[...]
        o_ref[...] = o
        acc_ref[...] = jnp.zeros_like(acc_ref)


def _peer(u, hnt, vt, e1, need, e2, rank, h, g, tt, rows, final_norm):
    d, t = hnt.shape
    ne = u.shape[0]
    n_a = rows // PEER_N_KEYS
    assert n_a % SUBLANES == 0 and d == rows
    per_block = ne // rows
    n_blocks = t // tt
    n_tiles = per_block * n_blocks
    tok = lambda n: jnp.minimum(n // per_block, n_blocks - 1)
    tok_prev = lambda n: jnp.maximum(n - 1, 0) // per_block
    kern = functools.partial(_peer_kernel, final_norm=final_norm, tiles_per_block=per_block)
    first = pl.BlockSpec((PEER_HEADS, n_a, tt), lambda n: (0, n % per_block, tok(n)))
    second = pl.BlockSpec((PEER_HEADS, PEER_N_KEYS, tt), lambda n: (0, 0, tok(n)))
    return pl.pallas_call(
        kern,
        grid=(n_tiles + 1,),
        in_specs=[
            pl.BlockSpec((rows, d), lambda n: (0, 0)),
            pl.BlockSpec((d, tt), lambda n: (0, 0)),
            pl.BlockSpec((rows, d), lambda n: ((n + 1) % per_block, 0)),
            pl.BlockSpec((d, tt), lambda n: (0, tok(n + 1))),
            pl.BlockSpec((d, rows), lambda n: (0, (n + per_block - 1) % per_block)),
            first, first, second, second,
            pl.BlockSpec((tt, d), lambda n: (tok_prev(n), 0)),
            pl.BlockSpec((1, d), lambda n: (0, 0)),
        ],
        out_specs=pl.BlockSpec((tt, d), lambda n: (tok_prev(n), 0)),
        out_shape=jax.ShapeDtypeStruct((t, d), F32),
        scratch_shapes=[
            pltpu.VMEM((d, tt), F32),
            pltpu.VMEM((rows, tt), F32),
            pltpu.VMEM((rows, tt), F32),
            pltpu.VMEM((rows, tt), BF16),
            pltpu.VMEM((rows, tt), BF16),
            pltpu.VMEM((2, PEER_HEADS, EW_ROWS, tt), BF16),
        ],
        compiler_params=pltpu.CompilerParams(
            dimension_semantics=("arbitrary",), vmem_limit_bytes=VMEM_LIMIT),
        name="peer",
    )(u, hnt, u, hnt, vt, e1, need, e2, rank, h, g)


def _largest_tile(n, cap):
    t = cap
    while n % t:
        t //= 2
    return t


def kernel(x, norm1_g, w_in, b_f, lambda_q1, lambda_k1, lambda_q2, lambda_k2, subln_g, w_out,
           norm2_g, peer_w_q, peer_sub_keys, peer_u, peer_v, final_g):
    b, s, d = x.shape
    depth = w_in.shape[0]
    t = b * s
    tile_s = _largest_tile(s, 512)
    tile_in = _largest_tile(s, 512)
    tile_route = _largest_tile(t, 256)
    tile_peer = _largest_tile(t, 512)
    rows_peer = 1024

    posq, poskt, slope_tab = _alibi_operands(s)
    h = x
    for layer in range(depth):
        w = w_in[layer]
        o = 0
        parts = {}
        for name, width in (("qa", FOX_WIDTH), ("ka", FOX_WIDTH), ("va", FOX_WIDTH), ("f", FOX_HEADS),
                            ("qb", DIFF_WIDTH), ("kb", DIFF_WIDTH), ("vb", DIFF_WIDTH)):
            parts[name] = w[:, o:o + width]
            o += width
        gate = jnp.zeros((d, LANES), F32)
        bf128 = jnp.zeros((1, LANES), F32)
        for g0 in GATE_GROUPS:
            gate = gate.at[:, g0:g0 + FOX_HEADS].set(parts["f"])
            bf128 = bf128.at[0, g0:g0 + FOX_HEADS].set(b_f[layer])
        va_pad = jnp.pad(parts["va"].reshape(d, FOX_HEADS, FOX_HEAD_DIM),
                         ((0, 0), (0, 0), (0, FOX_HEAD_DIM))).reshape(d, 2 * FOX_WIDTH)
        w_all = jnp.concatenate([parts["qa"], va_pad, parts["qb"], parts["vb"], gate], axis=1).astype(BF16)
        w_kt = jnp.concatenate([parts["ka"], parts["kb"]], axis=1).T.astype(BF16)

        qa, bq, kta, bkt, vaa, qb, ktb, vb = _inproj(h, norm1_g[layer][None, :], w_all, w_kt, bf128, tile_in)
        out_a = _fox(qa, bq, kta, bkt, vaa, tile_s)

        lambda_init = 0.8 - 0.6 * math.exp(-0.3 * layer)
        lam_rows = jnp.zeros((4, LANES), F32)
        for r, p in enumerate((lambda_q1, lambda_k1, lambda_q2, lambda_k2)):
            lam_rows = lam_rows.at[r, :DIFF_HALF_DIM].set(p[layer].astype(F32))
        out_b = _diff(qb, posq, ktb, poskt, vb, slope_tab, lam_rows, subln_g[layer][None, :].astype(F32),
                      tile_s, lambda_init)

        keys = peer_sub_keys[layer].reshape(2 * PEER_HEADS, PEER_N_KEYS, PEER_HALF_DIM).astype(BF16)
        h2, hnt, e1, need, e2, rank = _route(
            out_a.reshape(t, FOX_WIDTH), out_b.reshape(t, DIFF_WIDTH), h.reshape(t, d),
            w_out[layer].astype(BF16), norm2_g[layer][None, :], peer_w_q[layer].astype(BF16), keys,
            tile_route)

        last = layer == depth - 1
        h = _peer(peer_u[layer].astype(BF16), hnt, peer_v[layer].T.astype(BF16), e1, need, e2, rank,
                  h2, final_g[None, :], tile_peer, rows_peer, last).reshape(b, s, d)
    return h
```

```python
import functools
import math

import jax
import jax.numpy as jnp
from jax import lax
from jax.experimental import pallas as pl
from jax.experimental.pallas import tpu as pltpu

F32 = jnp.float32
BF16 = jnp.bfloat16

D_MODEL = 1024
FOX_HEADS = 8
FOX_HEAD_DIM = 64
FOX_WIDTH = FOX_HEADS * FOX_HEAD_DIM
DIFF_HEADS = 4
DIFF_HALF_DIM = 64
DIFF_HEAD_DIM = 2 * DIFF_HALF_DIM
DIFF_WIDTH = DIFF_HEADS * DIFF_HEAD_DIM
CHUNK = 64
PEER_HEADS = 8
PEER_N_KEYS = 128
PEER_N_EXPERTS = PEER_N_KEYS * PEER_N_KEYS
PEER_HALF_DIM = 128
PEER_TOPK = 16
RMS_EPS = 1e-6

LANES = 128
SUBLANES = 8
NEG_BIG = -1e30
VMEM_LIMIT = 56 * 1024 * 1024

GATE_GROUPS = (0, 8, 16, 32, 40, 48)

NT_DIMS = (((1,), (1,)), ((), ()))


def _split3(v):
    hi = v.astype(BF16)
    r1 = v - hi.astype(F32)
    mid = r1.astype(BF16)
    lo = (r1 - mid.astype(F32)).astype(BF16)
    return hi, mid, lo


def _inproj_kernel(x_ref, g_ref, w_ref, wkt_ref, bf_ref, tril_ref,
                   qa_ref, bq_ref, kta_ref, bkt_ref, va_ref, qb_ref, ktb_ref, vb_ref, carry_ref):
    tm = x_ref.shape[0]

    @pl.when(pl.program_id(1) == 0)
    def _():
        carry_ref[...] = jnp.zeros_like(carry_ref)

    x = x_ref[...]
    ms = jnp.mean(x * x, axis=-1, keepdims=True)
    xb = (x * lax.rsqrt(ms + RMS_EPS) * g_ref[...]).astype(BF16)
    proj = jnp.dot(xb, w_ref[...], preferred_element_type=F32)
    kt = lax.dot_general(wkt_ref[...], xb, NT_DIMS, preferred_element_type=F32)

    o_va = FOX_WIDTH
    o_qb = o_va + 2 * FOX_WIDTH
    o_vb = o_qb + DIFF_WIDTH
    o_gate = o_vb + DIFF_WIDTH
    qa_ref[...] = (proj[:, 0:FOX_WIDTH] * 0.125).astype(BF16)
    qb_ref[...] = (proj[:, o_qb:o_qb + DIFF_WIDTH] * 0.125).astype(BF16)
    vb_ref[...] = proj[:, o_vb:o_vb + DIFF_WIDTH].astype(BF16)
    kta_ref[...] = kt[0:FOX_WIDTH, :].astype(BF16)
    ktb_ref[...] = kt[FOX_WIDTH:FOX_WIDTH + DIFF_WIDTH, :].astype(BF16)
    va = proj[:, o_va:o_qb]
    vlane = lax.broadcasted_iota(jnp.int32, va.shape, 1)
    va_ref[...] = jnp.where((vlane & FOX_HEAD_DIM) != 0, 1.0, va).astype(BF16)

    z = proj[:, o_gate:o_gate + LANES] + bf_ref[...]
    logf = jnp.minimum(z, 0.0) - jnp.log1p(jnp.exp(-jnp.abs(z)))
    tril = tril_ref[...]
    hi, mid, lo = _split3(logf)
    c = (jnp.dot(tril, hi, preferred_element_type=F32)
         + jnp.dot(tril, mid, preferred_element_type=F32)
         + jnp.dot(tril, lo, preferred_element_type=F32)) + carry_ref[0:1, :]
    carry_ref[...] = jnp.broadcast_to(c[tm - 1:tm, :], carry_ref.shape)

    chi, cmid, clo = _split3(c)
    chi, cmid, clo = chi.astype(F32), cmid.astype(F32), clo.astype(F32)
    lane = lax.broadcasted_iota(jnp.int32, c.shape, 1)
    ones_hi = (lane >= 32) & (lane < 56)
    bq = jnp.where(lane < 8, chi, jnp.where(lane < 16, cmid, jnp.where(lane < 24, clo,
                   jnp.where(ones_hi, 1.0, 0.0))))
    bk = jnp.where(lane < 24, 1.0, jnp.where(lane < 32, 0.0, jnp.where(lane < 40, -chi,
                   jnp.where(lane < 48, -cmid, jnp.where(lane < 56, -clo, 0.0)))))
    bq_ref[...] = bq.astype(BF16)
    bkt_ref[...] = bk.T.astype(BF16)


def _inproj(x, g, w_all, w_kt, bf128, tm):
    b, s, d = x.shape
    ncol = w_all.shape[1]
    nkt = w_kt.shape[0]
    tril = jnp.tril(jnp.ones((tm, tm), F32)).astype(BF16)
    row = lambda bi, si: (bi, si, 0)
    col = lambda bi, si: (bi, 0, si)
    const = lambda bi, si: (0, 0)
    shapes = [
        ((b, s, FOX_WIDTH), pl.BlockSpec((None, tm, FOX_WIDTH), row)),
        ((b, s, LANES), pl.BlockSpec((None, tm, LANES), row)),
        ((b, FOX_WIDTH, s), pl.BlockSpec((None, FOX_WIDTH, tm), col)),
        ((b, LANES, s), pl.BlockSpec((None, LANES, tm), col)),
        ((b, s, 2 * FOX_WIDTH), pl.BlockSpec((None, tm, 2 * FOX_WIDTH), row)),
        ((b, s, DIFF_WIDTH), pl.BlockSpec((None, tm, DIFF_WIDTH), row)),
        ((b, DIFF_WIDTH, s), pl.BlockSpec((None, DIFF_WIDTH, tm), col)),
        ((b, s, DIFF_WIDTH), pl.BlockSpec((None, tm, DIFF_WIDTH), row)),
    ]
    return pl.pallas_call(
        _inproj_kernel,
        grid=(b, s // tm),
        in_specs=[
            pl.BlockSpec((None, tm, d), row),
            pl.BlockSpec((1, d), const),
            pl.BlockSpec((d, ncol), const),
            pl.BlockSpec((nkt, d), const),
            pl.BlockSpec((1, LANES), const),
            pl.BlockSpec((tm, tm), const),
        ],
        out_specs=[spec for _, spec in shapes],
        out_shape=[jax.ShapeDtypeStruct(shape, BF16) for shape, _ in shapes],
        scratch_shapes=[pltpu.VMEM((SUBLANES, LANES), F32)],
        compiler_params=pltpu.CompilerParams(
            dimension_semantics=("arbitrary", "arbitrary"), vmem_limit_bytes=VMEM_LIMIT),
        name="inproj",
    )(x, g, w_all, w_kt, bf128, tril)


def _softmax_update(s, m, acc, v_aug):
    m_new = jnp.maximum(m, jnp.max(s, axis=-1, keepdims=True))
    p = jnp.exp(s - m_new).astype(BF16)
    acc_new = jnp.exp(m - m_new) * acc + jnp.dot(p, v_aug, preferred_element_type=F32)
    return m_new, acc_new


def _fox_kernel(q_ref, bq_ref, kt_ref, bkt_ref, v0_ref, v1_ref, o_ref):
    tq = q_ref.shape[0]
    pair = pl.program_id(1)
    qi = pl.program_id(2)
    lane = lax.broadcasted_iota(jnp.int32, (tq, LANES), 1)
    qf = q_ref[...].astype(F32)
    bqf = bq_ref[...].astype(F32)
    qa = []
    for hh in range(2):
        head = 2 * pair + hh
        qm = jnp.where((lane >= FOX_HEAD_DIM * hh) & (lane < FOX_HEAD_DIM * (hh + 1)), qf, 0.0)
        bm = jnp.where(((lane & 7) == head) & (lane < 56), bqf, 0.0)
        qa.append(jnp.concatenate([qm, bm], axis=1).astype(BF16))
    q_all = jnp.concatenate(qa, axis=0)
    v_refs = (v0_ref, v1_ref)

    def step(j, carry, causal):
        ks = pl.multiple_of(j * tq, tq)
        kt = jnp.concatenate([kt_ref[:, pl.ds(ks, tq)], bkt_ref[:, pl.ds(ks, tq)]], axis=0)
        s_all = jnp.dot(q_all, kt, preferred_element_type=F32)
        out = []
        for hh in range(2):
            s = s_all[hh * tq:(hh + 1) * tq]
            if causal:
                row = lax.broadcasted_iota(jnp.int32, s.shape, 0)
                col = lax.broadcasted_iota(jnp.int32, s.shape, 1)
                s = jnp.where(col <= row, s, -jnp.inf)
            m, acc = carry[hh]
            out.append(_softmax_update(s, m, acc, v_refs[hh][pl.ds(ks, tq), :]))
        return tuple(out)

    init = tuple((jnp.full((tq, 1), NEG_BIG, F32), jnp.zeros((tq, LANES), F32)) for _ in range(2))
    carry = lax.fori_loop(0, qi, lambda j, c: step(j, c, False), init)
    (_, acc0), (_, acc1) = step(qi, carry, True)
    o0 = acc0 / pltpu.roll(acc0, FOX_HEAD_DIM, 1)
    o1 = pltpu.roll(acc1, FOX_HEAD_DIM, 1) / acc1
    o_ref[...] = jnp.where(lane < FOX_HEAD_DIM, o0, o1).astype(BF16)


def _fox(qa, bq, kta, bkt, vaa, tq):
    b, s, _ = qa.shape
    npair = FOX_HEADS // 2
    return pl.pallas_call(
        _fox_kernel,
        grid=(b, npair, s // tq),
        in_specs=[
            pl.BlockSpec((None, tq, LANES), lambda bi, p, i: (bi, i, p)),
            pl.BlockSpec((None, tq, LANES), lambda bi, p, i: (bi, i, 0)),
            pl.BlockSpec((None, LANES, s), lambda bi, p, i: (bi, p, 0)),
            pl.BlockSpec((None, LANES, s), lambda bi, p, i: (bi, 0, 0)),
            pl.BlockSpec((None, s, LANES), lambda bi, p, i: (bi, 0, 2 * p)),
            pl.BlockSpec((None, s, LANES), lambda bi, p, i: (bi, 0, 2 * p + 1)),
        ],
        out_specs=pl.BlockSpec((None, tq, LANES), lambda bi, p, i: (bi, i, p)),
        out_shape=jax.ShapeDtypeStruct((b, s, FOX_WIDTH), BF16),
        compiler_params=pltpu.CompilerParams(
            dimension_semantics=("parallel", "parallel", "arbitrary"), vmem_limit_bytes=VMEM_LIMIT),
        name="fox",
    )(qa, bq, kta, bkt, vaa, vaa)


def _diff_kernel(q_ref, pq_ref, kt_ref, pkt_ref, v_ref, slope_ref, lam_ref, g_ref, o_ref, *, out_scale,
                 lambda_init):
    tq = q_ref.shape[0]
    qi = pl.program_id(2)
    lane = lax.broadcasted_iota(jnp.int32, (tq, LANES), 1)
    qf = q_ref[...].astype(F32)
    pq = pq_ref[...]
    q1 = jnp.where(lane < DIFF_HALF_DIM, qf, 0.0).astype(BF16)
    q2 = jnp.where(lane >= DIFF_HALF_DIM, qf, 0.0).astype(BF16)
    q_all = jnp.concatenate([jnp.concatenate([q1, pq], axis=1),
                             jnp.concatenate([q2, pq], axis=1)], axis=0)
    ones = jnp.ones((tq, LANES), BF16)

    def step(j, carry, diagonal):
        ks = pl.multiple_of(j * tq, tq)
        kt = jnp.concatenate([kt_ref[:, pl.ds(ks, tq)], pkt_ref[:, pl.ds(ks, tq)]], axis=0)
        s_all = jnp.dot(q_all, kt, preferred_element_type=F32)
        v_aug = jnp.concatenate([v_ref[pl.ds(ks, tq), :], ones], axis=1)
        if diagonal:
            row = lax.broadcasted_iota(jnp.int32, (tq, tq), 0)
            col = lax.broadcasted_iota(jnp.int32, (tq, tq), 1)
            fix = jnp.where((col // CHUNK) <= (row // CHUNK),
                            (-2.0 * slope_ref[0:1, 0:1]) * jnp.maximum(col - row, 0).astype(F32), -jnp.inf)
        out = []
        for t in range(2):
            s = s_all[t * tq:(t + 1) * tq]
            if diagonal:
                s = s + fix
            m, acc = carry[t]
            out.append(_softmax_update(s, m, acc, v_aug))
        return tuple(out)

    init = tuple((jnp.full((tq, 1), NEG_BIG, F32), jnp.zeros((tq, 2 * LANES), F32)) for _ in range(2))
    carry = lax.fori_loop(0, qi, lambda j, c: step(j, c, False), init)
    (_, acc1), (_, acc2) = step(qi, carry, True)

    lp = lam_ref[...]
    lam = (jnp.exp(jnp.sum(lp[0:1] * lp[1:2], axis=-1, keepdims=True))
           - jnp.exp(jnp.sum(lp[2:3] * lp[3:4], axis=-1, keepdims=True)) + lambda_init)
    o = acc1[:, :LANES] / acc1[:, LANES:] - lam * (acc2[:, :LANES] / acc2[:, LANES:])
    ms = jnp.mean(o * o, axis=-1, keepdims=True)
    o = o * lax.rsqrt(ms + RMS_EPS) * g_ref[...]
    o_ref[...] = (o * out_scale).astype(BF16)


def _diff(qb, posq, ktb, poskt, vb, slopes, lam_rows, subln_g, tq, lambda_init):
    b, s, _ = qb.shape
    kern = functools.partial(_diff_kernel, out_scale=1.0 - lambda_init, lambda_init=lambda_init)
    return pl.pallas_call(
        kern,
        grid=(b, DIFF_HEADS, s // tq),
        in_specs=[
            pl.BlockSpec((None, tq, LANES), lambda bi, h, i: (bi, i, h)),
            pl.BlockSpec((None, tq, LANES), lambda bi, h, i: (h, i, 0)),
            pl.BlockSpec((None, LANES, s), lambda bi, h, i: (bi, h, 0)),
            pl.BlockSpec((LANES, s), lambda bi, h, i: (0, 0)),
            pl.BlockSpec((None, s, LANES), lambda bi, h, i: (bi, 0, h)),
            pl.BlockSpec((None, SUBLANES, LANES), lambda bi, h, i: (h, 0, 0)),
            pl.BlockSpec((4, LANES), lambda bi, h, i: (0, 0)),
            pl.BlockSpec((1, LANES), lambda bi, h, i: (0, 0)),
        ],
        out_specs=pl.BlockSpec((None, tq, LANES), lambda bi, h, i: (bi, i, h)),
        out_shape=jax.ShapeDtypeStruct((b, s, DIFF_WIDTH), BF16),
        compiler_params=pltpu.CompilerParams(
            dimension_semantics=("parallel", "parallel", "arbitrary"), vmem_limit_bytes=VMEM_LIMIT),
        name="diff",
    )(qb, posq, ktb, poskt, vb, slopes, lam_rows, subln_g)


def _alibi_operands(s):
    pos = jnp.arange(s, dtype=jnp.int32)
    hi = (pos // 64).astype(F32)
    lo = (pos % 64).astype(F32)
    slopes = jnp.asarray([2.0 ** (-8.0 * (i + 1) / DIFF_HEADS) for i in range(DIFF_HEADS)], F32)
    sl = slopes[:, None, None]
    lane = jnp.arange(LANES, dtype=jnp.int32)
    qcols = (-sl * 64.0 * hi[None, :, None], -sl * lo[None, :, None], sl * 64.0, sl)
    zq = jnp.zeros((DIFF_HEADS, s, LANES), F32)
    for i, col in enumerate(qcols):
        zq = jnp.where(lane == i, col, zq)
    kcols = (jnp.ones((1, s), F32), jnp.ones((1, s), F32), hi[None], lo[None])
    zk = jnp.zeros((LANES, s), F32)
    for i, col in enumerate(kcols):
        zk = jnp.where(lane[:, None] == i, col, zk)
    slope_tab = jnp.broadcast_to(slopes[:, None, None], (DIFF_HEADS, SUBLANES, LANES))
    return zq.astype(BF16), zk.astype(BF16), slope_tab


def _cmpx(xs, i, j):
    hi = jnp.maximum(xs[i], xs[j])
    lo = jnp.minimum(xs[i], xs[j])
    xs[i], xs[j] = hi, lo


def _bitonic_merge_desc(xs):
    n = len(xs)
    d = n // 2
    while d >= 1:
        for i in range(n):
            if (i & d) == 0:
                _cmpx(xs, i, i + d)
        d //= 2


def _sort_desc(xs):
    n = len(xs)
    k = 2
    while k <= n:
        d = k // 2
        while d >= 1:
            for i in range(n):
                l = i ^ d
                if l > i:
                    if (i & k) == 0:
                        _cmpx(xs, i, l)
                    else:
                        _cmpx(xs, l, i)
            d //= 2
        k *= 2


def _top16_desc(xs):
    xs = list(xs)
    _sort_desc(xs)
    for shift in (4, 2, 1):
        other = [pltpu.roll(xs[15 - i], shift, 0) for i in range(16)]
        xs = [jnp.maximum(xs[i], other[i]) for i in range(16)]
        _bitonic_merge_desc(xs)
    return xs


def _sublane_sum(x):
    for shift in (4, 2, 1):
        x = x + pltpu.roll(x, shift, 0)
    return x


def _route_kernel(oa_ref, ob_ref, x_ref, wo_ref, g_ref, wq_ref, keys_ref,
                  h_ref, hnt_ref, e1_ref, need_ref, e2_ref, rank_ref, sc_ref):
    tm = x_ref.shape[0]
    mixed = jnp.concatenate([oa_ref[...], ob_ref[...]], axis=1)
    h = x_ref[...] + jnp.dot(mixed, wo_ref[...], preferred_element_type=F32)
    h_ref[...] = h
    ms = jnp.mean(h * h, axis=-1, keepdims=True)
    hn_f = h * lax.rsqrt(ms + RMS_EPS) * g_ref[...]
    hn = hn_f.astype(BF16)
    hnt_ref[...] = hn_f.T.astype(BF16)
    q = jnp.dot(hn, wq_ref[...], preferred_element_type=F32).astype(BF16)
    for hp in range(2 * PEER_HEADS):
        qs = q[:, hp * PEER_HALF_DIM:(hp + 1) * PEER_HALF_DIM]
        sc_ref[hp] = lax.dot_general(keys_ref[hp], qs, NT_DIMS, preferred_element_type=F32)

    sub = lax.broadcasted_iota(jnp.int32, (SUBLANES, tm), 0)
    neg_inf = jnp.full((SUBLANES, tm), -jnp.inf, F32)
    groups = PEER_N_KEYS // SUBLANES

    def spread(vals):
        out = vals[SUBLANES - 1]
        for r in range(SUBLANES - 2, -1, -1):
            out = jnp.where(sub == r, vals[r], out)
        return out

    def head_body(hd, _):
        s1 = [sc_ref[2 * hd, g * SUBLANES:(g + 1) * SUBLANES, :] for g in range(groups)]
        s2 = [sc_ref[2 * hd + 1, g * SUBLANES:(g + 1) * SUBLANES, :] for g in range(groups)]
        a = _top16_desc(s1)
        b = _top16_desc(s2)
        b_lo, b_hi, a_hi = spread(b[:8]), spread(b[8:]), spread(a[8:])
        cands = [a[0] + b_lo, a[0] + b_hi, a[1] + b_lo]
        for i, n in ((2, 5), (3, 4), (4, 3), (5, 2), (6, 2), (7, 2)):
            cands.append(jnp.where(sub < n, a[i] + b_lo, neg_inf))
        cands.append(a_hi + b[0])
        top = _top16_desc(cands + [neg_inf] * (16 - len(cands)))
        tau = top[PEER_TOPK - 1]
        z = jnp.zeros((SUBLANES, tm), F32)
        for c in cands:
            z = z + jnp.where(c >= tau, jnp.exp(c - top[0]), 0.0)
        inv_z = 0.5 / _sublane_sum(z)
        for g0 in range(0, groups, 2):
            needs, ranks, e1s, e2s = [], [], [], []
            for g in (g0, g0 + 1):
                need = jnp.full((SUBLANES, tm), PEER_TOPK + 1.0, F32)
                rank = jnp.zeros((SUBLANES, tm), F32)
                for k in range(PEER_TOPK):
                    need = jnp.where(s1[g] + b[k] >= tau, float(PEER_TOPK - k), need)
                for k in reversed(range(PEER_TOPK)):
                    rank = jnp.where(s2[g] >= b[k], float(PEER_TOPK - k), rank)
                needs.append(need)
                ranks.append(rank)
                e1s.append(jnp.exp(s1[g] - a[0]))
                e2s.append(jnp.exp(s2[g] - b[0]) * inv_z)
            rows = pl.ds(g0 * SUBLANES, 2 * SUBLANES)
            e1_ref[hd, rows, :] = jnp.concatenate(e1s, axis=0)
            need_ref[hd, rows, :] = jnp.concatenate(needs, axis=0)
            e2_ref[hd, rows, :] = jnp.concatenate(e2s, axis=0).astype(BF16)
            rank_ref[hd, rows, :] = jnp.concatenate(ranks, axis=0).astype(BF16)
        return 0

    lax.fori_loop(0, PEER_HEADS, head_body, 0)


def _route(oa, ob, x2, wo, g2, wq, keys, tm):
    t, d = x2.shape
    nq = wq.shape[1]
    row = lambda i: (i, 0)
    const2 = lambda i: (0, 0)
    tok3 = lambda i: (0, 0, i)
    score_spec = pl.BlockSpec((PEER_HEADS, PEER_N_KEYS, tm), tok3)
    score_shape = jax.ShapeDtypeStruct((PEER_HEADS, PEER_N_KEYS, t), F32)
    return pl.pallas_call(
        _route_kernel,
        grid=(t // tm,),
        in_specs=[
            pl.BlockSpec((tm, FOX_WIDTH), row),
            pl.BlockSpec((tm, DIFF_WIDTH), row),
            pl.BlockSpec((tm, d), row),
            pl.BlockSpec((d, d), const2),
            pl.BlockSpec((1, d), const2),
            pl.BlockSpec((d, nq), const2),
            pl.BlockSpec((2 * PEER_HEADS, PEER_N_KEYS, PEER_HALF_DIM), lambda i: (0, 0, 0)),
        ],
        out_specs=[
            pl.BlockSpec((tm, d), row),
            pl.BlockSpec((d, tm), lambda i: (0, i)),
            score_spec, score_spec, score_spec, score_spec,
        ],
        out_shape=[
            jax.ShapeDtypeStruct((t, d), F32),
            jax.ShapeDtypeStruct((d, t), BF16),
            score_shape, score_shape,
            jax.ShapeDtypeStruct(score_shape.shape, BF16),
            jax.ShapeDtypeStruct(score_shape.shape, BF16),
        ],
        scratch_shapes=[pltpu.VMEM((2 * PEER_HEADS, PEER_N_KEYS, tm), F32)],
        compiler_params=pltpu.CompilerParams(
            dimension_semantics=("parallel",), vmem_limit_bytes=VMEM_LIMIT),
        name="route",
    )(oa, ob, x2, wo, g2, wq, keys)


def _gelu2(x):
    return x * (1.0 + lax.erf(x * (1.0 / math.sqrt(2.0))))


MXU_TILE = 256
EW_ROWS = 16


def _peer_kernel(u0_ref, xt0_ref, un_ref, xtn_ref, vtp_ref, e1_ref, need_ref, e2_ref, rank_ref, h_ref, g_ref,
                 o_ref, acc_ref, hta_ref, htb_ref, wa_ref, wb_ref, row_ref, *, final_norm, tiles_per_block):
    n = pl.program_id(0)
    rows = un_ref.shape[0]
    tt = xtn_ref.shape[1]
    n_a = rows // PEER_N_KEYS
    col_tiles = tt // MXU_TILE
    assert (rows // MXU_TILE) * col_tiles == n_a

    @pl.when(n == 0)
    def _():
        acc_ref[...] = jnp.zeros_like(acc_ref)
        wb_ref[...] = jnp.zeros_like(wb_ref)
        hta_ref[...] = jnp.dot(u0_ref[...], xt0_ref[...], preferred_element_type=F32)

    def run(ht_cur, ht_nxt, w_cur, w_prev, stages):
        for al in stages:
            rt = pl.ds((al // col_tiles) * MXU_TILE, MXU_TILE)
            ct = pl.ds((al % col_tiles) * MXU_TILE, MXU_TILE)
            for hd in range(PEER_HEADS):
                row_ref[0, hd] = jnp.broadcast_to(e1_ref[hd, al:al + 1, :], (EW_ROWS, tt)).astype(BF16)
                row_ref[1, hd] = jnp.broadcast_to(need_ref[hd, al:al + 1, :], (EW_ROWS, tt)).astype(BF16)
            for rc in range(0, PEER_N_KEYS // EW_ROWS, 2):
                tiles = [slice((rc + i) * EW_ROWS, (rc + i + 1) * EW_ROWS) for i in range(2)]
                w = [None, None]
                for hd in range(PEER_HEADS):
                    e1t = row_ref[0, hd]
                    needt = row_ref[1, hd]
                    for i, rs in enumerate(tiles):
                        gate = e1t * e2_ref[hd, rs, :]
                        part = jnp.where(rank_ref[hd, rs, :] >= needt, gate, jnp.zeros_like(gate))
                        w[i] = part if w[i] is None else w[i] + part
                for i in range(2):
                    rd = pl.ds(al * PEER_N_KEYS + (rc + i) * EW_ROWS, EW_ROWS)
                    w_cur[rd, :] = _gelu2(ht_cur[rd, :]).astype(BF16) * w[i]
            ht_nxt[rt, ct] = jnp.dot(un_ref[rt, :], xtn_ref[:, ct], preferred_element_type=F32)
            acc_ref[rt, ct] += jnp.dot(vtp_ref[rt, :], w_prev[:, ct], preferred_element_type=F32)

    @pl.when(n % 2 == 0)
    def _():
        run(hta_ref, htb_ref, wa_ref, wb_ref, range(n_a))

    @pl.when(n % 2 == 1)
    def _():
        run(htb_ref, hta_ref, wb_ref, wa_ref, range(n_a))

    @pl.when((n % tiles_per_block == 0) & (n > 0))
    def _():
        o = acc_ref[...].T + h_ref[...]
        if final_norm:
            ms = jnp.mean(o * o, axis=-1, keepdims=True)
            o = o * lax.rsqrt(ms + RMS_EPS) * g_ref[...]
        o_ref[...] = o
        acc_ref[...] = jnp.zeros_like(acc_ref)


def _peer(u, hnt, vt, e1, need, e2, rank, h, g, tt, rows, final_norm):
    d, t = hnt.shape
    ne = u.shape[0]
    n_a = rows // PEER_N_KEYS
    assert n_a % SUBLANES == 0 and d == rows
    per_block = ne // rows
    n_blocks = t // tt
    n_tiles = per_block * n_blocks
    tok = lambda n: jnp.minimum(n // per_block, n_blocks - 1)
    tok_prev = lambda n: jnp.maximum(n - 1, 0) // per_block
    kern = functools.partial(_peer_kernel, final_norm=final_norm, tiles_per_block=per_block)
    first = pl.BlockSpec((PEER_HEADS, n_a, tt), lambda n: (0, n % per_block, tok(n)))
    second = pl.BlockSpec((PEER_HEADS, PEER_N_KEYS, tt), lambda n: (0, 0, tok(n)))
    return pl.pallas_call(
        kern,
        grid=(n_tiles + 1,),
        in_specs=[
            pl.BlockSpec((rows, d), lambda n: (0, 0)),
            pl.BlockSpec((d, tt), lambda n: (0, 0)),
            pl.BlockSpec((rows, d), lambda n: ((n + 1) % per_block, 0)),
            pl.BlockSpec((d, tt), lambda n: (0, tok(n + 1))),
            pl.BlockSpec((d, rows), lambda n: (0, (n + per_block - 1) % per_block)),
            first, first, second, second,
            pl.BlockSpec((tt, d), lambda n: (tok_prev(n), 0)),
            pl.BlockSpec((1, d), lambda n: (0, 0)),
        ],
        out_specs=pl.BlockSpec((tt, d), lambda n: (tok_prev(n), 0)),
        out_shape=jax.ShapeDtypeStruct((t, d), F32),
        scratch_shapes=[
            pltpu.VMEM((d, tt), F32),
            pltpu.VMEM((rows, tt), F32),
            pltpu.VMEM((rows, tt), F32),
            pltpu.VMEM((rows, tt), BF16),
            pltpu.VMEM((rows, tt), BF16),
            pltpu.VMEM((2, PEER_HEADS, EW_ROWS, tt), BF16),
        ],
        compiler_params=pltpu.CompilerParams(
            dimension_semantics=("arbitrary",), vmem_limit_bytes=VMEM_LIMIT),
        name="peer",
    )(u, hnt, u, hnt, vt, e1, need, e2, rank, h, g)


def _largest_tile(n, cap):
    t = cap
    while n % t:
        t //= 2
    return t


def kernel(x, norm1_g, w_in, b_f, lambda_q1, lambda_k1, lambda_q2, lambda_k2, subln_g, w_out,
           norm2_g, peer_w_q, peer_sub_keys, peer_u, peer_v, final_g):
    b, s, d = x.shape
    depth = w_in.shape[0]
    t = b * s
    tile_s = _largest_tile(s, 512)
    tile_in = _largest_tile(s, 512)
    tile_route = _largest_tile(t, 256)
    tile_peer = _largest_tile(t, 512)
    rows_peer = 1024

    posq, poskt, slope_tab = _alibi_operands(s)
    h = x
    for layer in range(depth):
        w = w_in[layer]
        o = 0
        parts = {}
        for name, width in (("qa", FOX_WIDTH), ("ka", FOX_WIDTH), ("va", FOX_WIDTH), ("f", FOX_HEADS),
                            ("qb", DIFF_WIDTH), ("kb", DIFF_WIDTH), ("vb", DIFF_WIDTH)):
            parts[name] = w[:, o:o + width]
            o += width
        group_of = [g in GATE_GROUPS for g in range(0, LANES, FOX_HEADS)]
        gate = jnp.concatenate([parts["f"] if on else jnp.zeros((d, FOX_HEADS), F32) for on in group_of], axis=1)
        bf128 = jnp.concatenate([b_f[layer].astype(F32) if on else jnp.zeros((FOX_HEADS,), F32)
                                 for on in group_of])[None, :]
        va_pad = jnp.pad(parts["va"].reshape(d, FOX_HEADS, FOX_HEAD_DIM),
                         ((0, 0), (0, 0), (0, FOX_HEAD_DIM))).reshape(d, 2 * FOX_WIDTH)
        w_all = jnp.concatenate([parts["qa"], va_pad, parts["qb"], parts["vb"], gate], axis=1).astype(BF16)
        w_kt = jnp.concatenate([parts["ka"], parts["kb"]], axis=1).T.astype(BF16)

        qa, bq, kta, bkt, vaa, qb, ktb, vb = _inproj(h, norm1_g[layer][None, :], w_all, w_kt, bf128, tile_in)
        out_a = _fox(qa, bq, kta, bkt, vaa, tile_s)

        lambda_init = 0.8 - 0.6 * math.exp(-0.3 * layer)
        lam_rows = jnp.zeros((4, LANES), F32)
        for r, p in enumerate((lambda_q1, lambda_k1, lambda_q2, lambda_k2)):
            lam_rows = lam_rows.at[r, :DIFF_HALF_DIM].set(p[layer].astype(F32))
        out_b = _diff(qb, posq, ktb, poskt, vb, slope_tab, lam_rows, subln_g[layer][None, :].astype(F32),
                      tile_s, lambda_init)

        keys = peer_sub_keys[layer].reshape(2 * PEER_HEADS, PEER_N_KEYS, PEER_HALF_DIM).astype(BF16)
        h2, hnt, e1, need, e2, rank = _route(
            out_a.reshape(t, FOX_WIDTH), out_b.reshape(t, DIFF_WIDTH), h.reshape(t, d),
            w_out[layer].astype(BF16), norm2_g[layer][None, :], peer_w_q[layer].astype(BF16), keys,
            tile_route)

        last = layer == depth - 1
        h = _peer(peer_u[layer].astype(BF16), hnt, peer_v[layer].T.astype(BF16), e1, need, e2, rank,
                  h2, final_g[None, :], tile_peer, rows_peer, last).reshape(b, s, d)
    return h
```

```python
import functools
import math

import jax
import jax.numpy as jnp
from jax import lax
from jax.experimental import pallas as pl
from jax.experimental.pallas import tpu as pltpu

F32 = jnp.float32
BF16 = jnp.bfloat16

D_MODEL = 1024
FOX_HEADS = 8
FOX_HEAD_DIM = 64
FOX_WIDTH = FOX_HEADS * FOX_HEAD_DIM
DIFF_HEADS = 4
DIFF_HALF_DIM = 64
DIFF_HEAD_DIM = 2 * DIFF_HALF_DIM
DIFF_WIDTH = DIFF_HEADS * DIFF_HEAD_DIM
CHUNK = 64
PEER_HEADS = 8
PEER_N_KEYS = 128
PEER_N_EXPERTS = PEER_N_KEYS * PEER_N_KEYS
PEER_HALF_DIM = 128
PEER_TOPK = 16
RMS_EPS = 1e-6

LANES = 128
SUBLANES = 8
NEG_BIG = -1e30
VMEM_LIMIT = 56 * 1024 * 1024

GATE_GROUPS = (0, 8, 16, 32, 40, 48)

NT_DIMS = (((1,), (1,)), ((), ()))


def _split3(v):
    hi = v.astype(BF16)
    r1 = v - hi.astype(F32)
    mid = r1.astype(BF16)
    lo = (r1 - mid.astype(F32)).astype(BF16)
    return hi, mid, lo


def _inproj_kernel(x_ref, g_ref, w_ref, wkt_ref, bf_ref, tril_ref,
                   qa_ref, bq_ref, kta_ref, bkt_ref, va_ref, qb_ref, ktb_ref, vb_ref, carry_ref):
    tm = x_ref.shape[0]

    @pl.when(pl.program_id(1) == 0)
    def _():
        carry_ref[...] = jnp.zeros_like(carry_ref)

    x = x_ref[...]
    ms = jnp.mean(x * x, axis=-1, keepdims=True)
    xb = (x * lax.rsqrt(ms + RMS_EPS) * g_ref[...]).astype(BF16)
    proj = jnp.dot(xb, w_ref[...], preferred_element_type=F32)
    kt = lax.dot_general(wkt_ref[...], xb, NT_DIMS, preferred_element_type=F32)

    o_va = FOX_WIDTH
    o_qb = o_va + 2 * FOX_WIDTH
    o_vb = o_qb + DIFF_WIDTH
    o_gate = o_vb + DIFF_WIDTH
    qa_ref[...] = (proj[:, 0:FOX_WIDTH] * 0.125).astype(BF16)
    qb_ref[...] = (proj[:, o_qb:o_qb + DIFF_WIDTH] * 0.125).astype(BF16)
    vb_ref[...] = proj[:, o_vb:o_vb + DIFF_WIDTH].astype(BF16)
    kta_ref[...] = kt[0:FOX_WIDTH, :].astype(BF16)
    ktb_ref[...] = kt[FOX_WIDTH:FOX_WIDTH + DIFF_WIDTH, :].astype(BF16)
    va = proj[:, o_va:o_qb]
    vlane = lax.broadcasted_iota(jnp.int32, va.shape, 1)
    va_ref[...] = jnp.where((vlane & FOX_HEAD_DIM) != 0, 1.0, va).astype(BF16)

    z = proj[:, o_gate:o_gate + LANES] + bf_ref[...]
    logf = jnp.minimum(z, 0.0) - jnp.log1p(jnp.exp(-jnp.abs(z)))
    tril = tril_ref[...]
    hi, mid, lo = _split3(logf)
    c = (jnp.dot(tril, hi, preferred_element_type=F32)
         + jnp.dot(tril, mid, preferred_element_type=F32)
         + jnp.dot(tril, lo, preferred_element_type=F32)) + carry_ref[0:1, :]
    carry_ref[...] = jnp.broadcast_to(c[tm - 1:tm, :], carry_ref.shape)

    chi, cmid, clo = _split3(c)
    chi, cmid, clo = chi.astype(F32), cmid.astype(F32), clo.astype(F32)
    lane = lax.broadcasted_iota(jnp.int32, c.shape, 1)
    ones_hi = (lane >= 32) & (lane < 56)
    bq = jnp.where(lane < 8, chi, jnp.where(lane < 16, cmid, jnp.where(lane < 24, clo,
                   jnp.where(ones_hi, 1.0, 0.0))))
    bk = jnp.where(lane < 24, 1.0, jnp.where(lane < 32, 0.0, jnp.where(lane < 40, -chi,
                   jnp.where(lane < 48, -cmid, jnp.where(lane < 56, -clo, 0.0)))))
    bq_ref[...] = bq.astype(BF16)
    bkt_ref[...] = bk.T.astype(BF16)


def _inproj(x, g, w_all, w_kt, bf128, tm):
    b, s, d = x.shape
    ncol = w_all.shape[1]
    nkt = w_kt.shape[0]
    tril = jnp.tril(jnp.ones((tm, tm), F32)).astype(BF16)
    row = lambda bi, si: (bi, si, 0)
    col = lambda bi, si: (bi, 0, si)
    const = lambda bi, si: (0, 0)
    shapes = [
        ((b, s, FOX_WIDTH), pl.BlockSpec((None, tm, FOX_WIDTH), row)),
        ((b, s, LANES), pl.BlockSpec((None, tm, LANES), row)),
        ((b, FOX_WIDTH, s), pl.BlockSpec((None, FOX_WIDTH, tm), col)),
        ((b, LANES, s), pl.BlockSpec((None, LANES, tm), col)),
        ((b, s, 2 * FOX_WIDTH), pl.BlockSpec((None, tm, 2 * FOX_WIDTH), row)),
        ((b, s, DIFF_WIDTH), pl.BlockSpec((None, tm, DIFF_WIDTH), row)),
        ((b, DIFF_WIDTH, s), pl.BlockSpec((None, DIFF_WIDTH, tm), col)),
        ((b, s, DIFF_WIDTH), pl.BlockSpec((None, tm, DIFF_WIDTH), row)),
    ]
    return pl.pallas_call(
        _inproj_kernel,
        grid=(b, s // tm),
        in_specs=[
            pl.BlockSpec((None, tm, d), row),
            pl.BlockSpec((1, d), const),
            pl.BlockSpec((d, ncol), const),
            pl.BlockSpec((nkt, d), const),
            pl.BlockSpec((1, LANES), const),
            pl.BlockSpec((tm, tm), const),
        ],
        out_specs=[spec for _, spec in shapes],
        out_shape=[jax.ShapeDtypeStruct(shape, BF16) for shape, _ in shapes],
        scratch_shapes=[pltpu.VMEM((SUBLANES, LANES), F32)],
        compiler_params=pltpu.CompilerParams(
            dimension_semantics=("arbitrary", "arbitrary"), vmem_limit_bytes=VMEM_LIMIT),
        name="inproj",
    )(x, g, w_all, w_kt, bf128, tril)


def _softmax_update(s, m, acc, v_aug):
    m_new = jnp.maximum(m, jnp.max(s, axis=-1, keepdims=True))
    p = jnp.exp(s - m_new).astype(BF16)
    acc_new = jnp.exp(m - m_new) * acc + jnp.dot(p, v_aug, preferred_element_type=F32)
    return m_new, acc_new


def _fox_kernel(q_ref, bq_ref, kt_ref, bkt_ref, v0_ref, v1_ref, o_ref):
    tq = q_ref.shape[0]
    pair = pl.program_id(1)
    qi = pl.program_id(2)
    lane = lax.broadcasted_iota(jnp.int32, (tq, LANES), 1)
    qf = q_ref[...].astype(F32)
    bqf = bq_ref[...].astype(F32)
    qa = []
    for hh in range(2):
        head = 2 * pair + hh
        qm = jnp.where((lane >= FOX_HEAD_DIM * hh) & (lane < FOX_HEAD_DIM * (hh + 1)), qf, 0.0)
        bm = jnp.where(((lane & 7) == head) & (lane < 56), bqf, 0.0)
        qa.append(jnp.concatenate([qm, bm], axis=1).astype(BF16))
    q_all = jnp.concatenate(qa, axis=0)
    v_refs = (v0_ref, v1_ref)

    def step(j, carry, causal):
        ks = pl.multiple_of(j * tq, tq)
        kt = jnp.concatenate([kt_ref[:, pl.ds(ks, tq)], bkt_ref[:, pl.ds(ks, tq)]], axis=0)
        s_all = jnp.dot(q_all, kt, preferred_element_type=F32)
        out = []
        for hh in range(2):
            s = s_all[hh * tq:(hh + 1) * tq]
            if causal:
                row = lax.broadcasted_iota(jnp.int32, s.shape, 0)
                col = lax.broadcasted_iota(jnp.int32, s.shape, 1)
                s = jnp.where(col <= row, s, -jnp.inf)
            m, acc = carry[hh]
            out.append(_softmax_update(s, m, acc, v_refs[hh][pl.ds(ks, tq), :]))
        return tuple(out)

    init = tuple((jnp.full((tq, 1), NEG_BIG, F32), jnp.zeros((tq, LANES), F32)) for _ in range(2))
    carry = lax.fori_loop(0, qi, lambda j, c: step(j, c, False), init)
    (_, acc0), (_, acc1) = step(qi, carry, True)
    o0 = acc0 / pltpu.roll(acc0, FOX_HEAD_DIM, 1)
    o1 = pltpu.roll(acc1, FOX_HEAD_DIM, 1) / acc1
    o_ref[...] = jnp.where(lane < FOX_HEAD_DIM, o0, o1).astype(BF16)


def _fox(qa, bq, kta, bkt, vaa, tq):
    b, s, _ = qa.shape
    npair = FOX_HEADS // 2
    return pl.pallas_call(
        _fox_kernel,
        grid=(b, npair, s // tq),
        in_specs=[
            pl.BlockSpec((None, tq, LANES), lambda bi, p, i: (bi, i, p)),
            pl.BlockSpec((None, tq, LANES), lambda bi, p, i: (bi, i, 0)),
            pl.BlockSpec((None, LANES, s), lambda bi, p, i: (bi, p, 0)),
            pl.BlockSpec((None, LANES, s), lambda bi, p, i: (bi, 0, 0)),
            pl.BlockSpec((None, s, LANES), lambda bi, p, i: (bi, 0, 2 * p)),
            pl.BlockSpec((None, s, LANES), lambda bi, p, i: (bi, 0, 2 * p + 1)),
        ],
        out_specs=pl.BlockSpec((None, tq, LANES), lambda bi, p, i: (bi, i, p)),
        out_shape=jax.ShapeDtypeStruct((b, s, FOX_WIDTH), BF16),
        compiler_params=pltpu.CompilerParams(
            dimension_semantics=("parallel", "parallel", "arbitrary"), vmem_limit_bytes=VMEM_LIMIT),
        name="fox",
    )(qa, bq, kta, bkt, vaa, vaa)


def _diff_kernel(q_ref, pq_ref, kt_ref, pkt_ref, v_ref, slope_ref, lam_ref, g_ref, o_ref, *, out_scale,
                 lambda_init):
    tq = q_ref.shape[0]
    qi = pl.program_id(2)
    lane = lax.broadcasted_iota(jnp.int32, (tq, LANES), 1)
    qf = q_ref[...].astype(F32)
    pq = pq_ref[...]
    q1 = jnp.where(lane < DIFF_HALF_DIM, qf, 0.0).astype(BF16)
    q2 = jnp.where(lane >= DIFF_HALF_DIM, qf, 0.0).astype(BF16)
    q_all = jnp.concatenate([jnp.concatenate([q1, pq], axis=1),
                             jnp.concatenate([q2, pq], axis=1)], axis=0)
    ones = jnp.ones((tq, LANES), BF16)

    def step(j, carry, diagonal):
        ks = pl.multiple_of(j * tq, tq)
        kt = jnp.concatenate([kt_ref[:, pl.ds(ks, tq)], pkt_ref[:, pl.ds(ks, tq)]], axis=0)
        s_all = jnp.dot(q_all, kt, preferred_element_type=F32)
        v_aug = jnp.concatenate([v_ref[pl.ds(ks, tq), :], ones], axis=1)
        if diagonal:
            row = lax.broadcasted_iota(jnp.int32, (tq, tq), 0)
            col = lax.broadcasted_iota(jnp.int32, (tq, tq), 1)
            fix = jnp.where((col // CHUNK) <= (row // CHUNK),
                            (-2.0 * slope_ref[0:1, 0:1]) * jnp.maximum(col - row, 0).astype(F32), -jnp.inf)
        out = []
        for t in range(2):
            s = s_all[t * tq:(t + 1) * tq]
            if diagonal:
                s = s + fix
            m, acc = carry[t]
            out.append(_softmax_update(s, m, acc, v_aug))
        return tuple(out)

    init = tuple((jnp.full((tq, 1), NEG_BIG, F32), jnp.zeros((tq, 2 * LANES), F32)) for _ in range(2))
    carry = lax.fori_loop(0, qi, lambda j, c: step(j, c, False), init)
    (_, acc1), (_, acc2) = step(qi, carry, True)

    lp = lam_ref[...]
    lam = (jnp.exp(jnp.sum(lp[0:1] * lp[1:2], axis=-1, keepdims=True))
           - jnp.exp(jnp.sum(lp[2:3] * lp[3:4], axis=-1, keepdims=True)) + lambda_init)
    o = acc1[:, :LANES] / acc1[:, LANES:] - lam * (acc2[:, :LANES] / acc2[:, LANES:])
    ms = jnp.mean(o * o, axis=-1, keepdims=True)
    o = o * lax.rsqrt(ms + RMS_EPS) * g_ref[...]
    o_ref[...] = (o * out_scale).astype(BF16)


def _diff(qb, posq, ktb, poskt, vb, slopes, lam_rows, subln_g, tq, lambda_init):
    b, s, _ = qb.shape
    kern = functools.partial(_diff_kernel, out_scale=1.0 - lambda_init, lambda_init=lambda_init)
    return pl.pallas_call(
        kern,
        grid=(b, DIFF_HEADS, s // tq),
        in_specs=[
            pl.BlockSpec((None, tq, LANES), lambda bi, h, i: (bi, i, h)),
            pl.BlockSpec((None, tq, LANES), lambda bi, h, i: (h, i, 0)),
            pl.BlockSpec((None, LANES, s), lambda bi, h, i: (bi, h, 0)),
            pl.BlockSpec((LANES, s), lambda bi, h, i: (0, 0)),
            pl.BlockSpec((None, s, LANES), lambda bi, h, i: (bi, 0, h)),
            pl.BlockSpec((None, SUBLANES, LANES), lambda bi, h, i: (h, 0, 0)),
            pl.BlockSpec((4, LANES), lambda bi, h, i: (0, 0)),
            pl.BlockSpec((1, LANES), lambda bi, h, i: (0, 0)),
        ],
        out_specs=pl.BlockSpec((None, tq, LANES), lambda bi, h, i: (bi, i, h)),
        out_shape=jax.ShapeDtypeStruct((b, s, DIFF_WIDTH), BF16),
        compiler_params=pltpu.CompilerParams(
            dimension_semantics=("parallel", "parallel", "arbitrary"), vmem_limit_bytes=VMEM_LIMIT),
        name="diff",
    )(qb, posq, ktb, poskt, vb, slopes, lam_rows, subln_g)


def _alibi_operands(s):
    pos = jnp.arange(s, dtype=jnp.int32)
    hi = (pos // 64).astype(F32)
    lo = (pos % 64).astype(F32)
    slopes = jnp.asarray([2.0 ** (-8.0 * (i + 1) / DIFF_HEADS) for i in range(DIFF_HEADS)], F32)
    sl = slopes[:, None, None]
    lane = jnp.arange(LANES, dtype=jnp.int32)
    qcols = (-sl * 64.0 * hi[None, :, None], -sl * lo[None, :, None], sl * 64.0, sl)
    zq = jnp.zeros((DIFF_HEADS, s, LANES), F32)
    for i, col in enumerate(qcols):
        zq = jnp.where(lane == i, col, zq)
    kcols = (jnp.ones((1, s), F32), jnp.ones((1, s), F32), hi[None], lo[None])
    zk = jnp.zeros((LANES, s), F32)
    for i, col in enumerate(kcols):
        zk = jnp.where(lane[:, None] == i, col, zk)
    slope_tab = jnp.broadcast_to(slopes[:, None, None], (DIFF_HEADS, SUBLANES, LANES))
    return zq.astype(BF16), zk.astype(BF16), slope_tab


def _cmpx(xs, i, j):
    hi = jnp.maximum(xs[i], xs[j])
    lo = jnp.minimum(xs[i], xs[j])
    xs[i], xs[j] = hi, lo


def _bitonic_merge_desc(xs):
    n = len(xs)
    d = n // 2
    while d >= 1:
        for i in range(n):
            if (i & d) == 0:
                _cmpx(xs, i, i + d)
        d //= 2


def _sort_desc(xs):
    n = len(xs)
    k = 2
    while k <= n:
        d = k // 2
        while d >= 1:
            for i in range(n):
                l = i ^ d
                if l > i:
                    if (i & k) == 0:
                        _cmpx(xs, i, l)
                    else:
                        _cmpx(xs, l, i)
            d //= 2
        k *= 2


def _top16_desc(xs):
    xs = list(xs)
    _sort_desc(xs)
    for shift in (4, 2, 1):
        other = [pltpu.roll(xs[15 - i], shift, 0) for i in range(16)]
        xs = [jnp.maximum(xs[i], other[i]) for i in range(16)]
        _bitonic_merge_desc(xs)
    return xs


def _sublane_sum(x):
    for shift in (4, 2, 1):
        x = x + pltpu.roll(x, shift, 0)
    return x


def _route_kernel(oa_ref, ob_ref, x_ref, wo_ref, g_ref, wq_ref, keys_ref,
                  h_ref, hnt_ref, e1_ref, need_ref, e2_ref, rank_ref, sc_ref):
    tm = x_ref.shape[0]
    mixed = jnp.concatenate([oa_ref[...], ob_ref[...]], axis=1)
    h = x_ref[...] + jnp.dot(mixed, wo_ref[...], preferred_element_type=F32)
    h_ref[...] = h
    ms = jnp.mean(h * h, axis=-1, keepdims=True)
    hn_f = h * lax.rsqrt(ms + RMS_EPS) * g_ref[...]
    hn = hn_f.astype(BF16)
    hnt_ref[...] = hn_f.T.astype(BF16)
    q = jnp.dot(hn, wq_ref[...], preferred_element_type=F32).astype(BF16)
    for hp in range(2 * PEER_HEADS):
        qs = q[:, hp * PEER_HALF_DIM:(hp + 1) * PEER_HALF_DIM]
        sc_ref[hp] = lax.dot_general(keys_ref[hp], qs, NT_DIMS, preferred_element_type=F32)

    sub = lax.broadcasted_iota(jnp.int32, (SUBLANES, tm), 0)
    neg_inf = jnp.full((SUBLANES, tm), -jnp.inf, F32)
    groups = PEER_N_KEYS // SUBLANES

    def spread(vals):
        out = vals[SUBLANES - 1]
        for r in range(SUBLANES - 2, -1, -1):
            out = jnp.where(sub == r, vals[r], out)
        return out

    def head_body(hd, _):
        s1 = [sc_ref[2 * hd, g * SUBLANES:(g + 1) * SUBLANES, :] for g in range(groups)]
        s2 = [sc_ref[2 * hd + 1, g * SUBLANES:(g + 1) * SUBLANES, :] for g in range(groups)]
        a = _top16_desc(s1)
        b = _top16_desc(s2)
        b_lo, b_hi, a_hi = spread(b[:8]), spread(b[8:]), spread(a[8:])
        cands = [a[0] + b_lo, a[0] + b_hi, a[1] + b_lo]
        for i, n in ((2, 5), (3, 4), (4, 3), (5, 2), (6, 2), (7, 2)):
            cands.append(jnp.where(sub < n, a[i] + b_lo, neg_inf))
        cands.append(a_hi + b[0])
        top = _top16_desc(cands + [neg_inf] * (16 - len(cands)))
        tau = top[PEER_TOPK - 1]
        z = jnp.zeros((SUBLANES, tm), F32)
        for c in cands:
            z = z + jnp.where(c >= tau, jnp.exp(c - top[0]), 0.0)
        inv_z = 0.5 / _sublane_sum(z)
        for g0 in range(0, groups, 2):
            needs, ranks, e1s, e2s = [], [], [], []
            for g in (g0, g0 + 1):
                need = jnp.full((SUBLANES, tm), PEER_TOPK + 1.0, F32)
                rank = jnp.zeros((SUBLANES, tm), F32)
                for k in range(PEER_TOPK):
                    need = jnp.where(s1[g] + b[k] >= tau, float(PEER_TOPK - k), need)
                for k in reversed(range(PEER_TOPK)):
                    rank = jnp.where(s2[g] >= b[k], float(PEER_TOPK - k), rank)
                needs.append(need)
                ranks.append(rank)
                e1s.append(jnp.exp(s1[g] - a[0]))
                e2s.append(jnp.exp(s2[g] - b[0]) * inv_z)
            rows = pl.ds(g0 * SUBLANES, 2 * SUBLANES)
            e1_ref[hd, rows, :] = jnp.concatenate(e1s, axis=0)
            need_ref[hd, rows, :] = jnp.concatenate(needs, axis=0)
            e2_ref[hd, rows, :] = jnp.concatenate(e2s, axis=0).astype(BF16)
            rank_ref[hd, rows, :] = jnp.concatenate(ranks, axis=0).astype(BF16)
        return 0

    lax.fori_loop(0, PEER_HEADS, head_body, 0)


def _route(oa, ob, x2, wo, g2, wq, keys, tm):
    t, d = x2.shape
    nq = wq.shape[1]
    row = lambda i: (i, 0)
    const2 = lambda i: (0, 0)
    tok3 = lambda i: (0, 0, i)
    score_spec = pl.BlockSpec((PEER_HEADS, PEER_N_KEYS, tm), tok3)
    score_shape = jax.ShapeDtypeStruct((PEER_HEADS, PEER_N_KEYS, t), F32)
    return pl.pallas_call(
        _route_kernel,
        grid=(t // tm,),
        in_specs=[
            pl.BlockSpec((tm, FOX_WIDTH), row),
            pl.BlockSpec((tm, DIFF_WIDTH), row),
            pl.BlockSpec((tm, d), row),
            pl.BlockSpec((d, d), const2),
            pl.BlockSpec((1, d), const2),
            pl.BlockSpec((d, nq), const2),
            pl.BlockSpec((2 * PEER_HEADS, PEER_N_KEYS, PEER_HALF_DIM), lambda i: (0, 0, 0)),
        ],
        out_specs=[
            pl.BlockSpec((tm, d), row),
            pl.BlockSpec((d, tm), lambda i: (0, i)),
            score_spec, score_spec, score_spec, score_spec,
        ],
        out_shape=[
            jax.ShapeDtypeStruct((t, d), F32),
            jax.ShapeDtypeStruct((d, t), BF16),
            score_shape, score_shape,
            jax.ShapeDtypeStruct(score_shape.shape, BF16),
            jax.ShapeDtypeStruct(score_shape.shape, BF16),
        ],
        scratch_shapes=[pltpu.VMEM((2 * PEER_HEADS, PEER_N_KEYS, tm), F32)],
        compiler_params=pltpu.CompilerParams(
            dimension_semantics=("parallel",), vmem_limit_bytes=VMEM_LIMIT),
        name="route",
    )(oa, ob, x2, wo, g2, wq, keys)


def _gelu2(x):
    return x * (1.0 + lax.erf(x * (1.0 / math.sqrt(2.0))))


MXU_TILE = 256
EW_ROWS = 16


def _peer_kernel(u0_ref, xt0_ref, un_ref, xtn_ref, vtp_ref, e1_ref, need_ref, e2_ref, rank_ref, h_ref, g_ref,
                 o_ref, acc_ref, hta_ref, htb_ref, wa_ref, wb_ref, row_ref, *, final_norm, tiles_per_block):
    n = pl.program_id(0)
    rows = un_ref.shape[0]
    tt = xtn_ref.shape[1]
    n_a = rows // PEER_N_KEYS
    col_tiles = tt // MXU_TILE
    assert (rows // MXU_TILE) * col_tiles == n_a

    @pl.when(n == 0)
    def _():
        acc_ref[...] = jnp.zeros_like(acc_ref)
        wb_ref[...] = jnp.zeros_like(wb_ref)
        hta_ref[...] = jnp.dot(u0_ref[...], xt0_ref[...], preferred_element_type=F32)

    def run(ht_cur, ht_nxt, w_cur, w_prev, stages):
        for al in stages:
            rt = pl.ds((al // col_tiles) * MXU_TILE, MXU_TILE)
            ct = pl.ds((al % col_tiles) * MXU_TILE, MXU_TILE)
            for hd in range(PEER_HEADS):
                row_ref[0, hd] = jnp.broadcast_to(e1_ref[hd, al:al + 1, :], (EW_ROWS, tt)).astype(BF16)
                row_ref[1, hd] = jnp.broadcast_to(need_ref[hd, al:al + 1, :], (EW_ROWS, tt)).astype(BF16)
            for rc in range(0, PEER_N_KEYS // EW_ROWS, 2):
                tiles = [slice((rc + i) * EW_ROWS, (rc + i + 1) * EW_ROWS) for i in range(2)]
                w = [None, None]
                for hd in range(PEER_HEADS):
                    e1t = row_ref[0, hd]
                    needt = row_ref[1, hd]
                    for i, rs in enumerate(tiles):
                        e2t = e2_ref[hd, rs, :]
                        part = e1t * jnp.where(rank_ref[hd, rs, :] >= needt, e2t, jnp.zeros_like(e2t))
                        w[i] = part if w[i] is None else w[i] + part
                for i in range(2):
                    rd = pl.ds(al * PEER_N_KEYS + (rc + i) * EW_ROWS, EW_ROWS)
                    w_cur[rd, :] = _gelu2(ht_cur[rd, :]).astype(BF16) * w[i]
            ht_nxt[rt, ct] = jnp.dot(un_ref[rt, :], xtn_ref[:, ct], preferred_element_type=F32)
            acc_ref[rt, ct] += jnp.dot(vtp_ref[rt, :], w_prev[:, ct], preferred_element_type=F32)

    @pl.when(n % 2 == 0)
    def _():
        run(hta_ref, htb_ref, wa_ref, wb_ref, range(n_a))

    @pl.when(n % 2 == 1)
    def _():
        run(htb_ref, hta_ref, wb_ref, wa_ref, range(n_a))

    @pl.when((n % tiles_per_block == 0) & (n > 0))
    def _():
        o = acc_ref[...].T + h_ref[...]
        if final_norm:
            ms = jnp.mean(o * o, axis=-1, keepdims=True)
            o = o * lax.rsqrt(ms + RMS_EPS) * g_ref[...]
        o_ref[...] = o
        acc_ref[...] = jnp.zeros_like(acc_ref)


def _peer(u, hnt, vt, e1, need, e2, rank, h, g, tt, rows, final_norm):
    d, t = hnt.shape
    ne = u.shape[0]
    n_a = rows // PEER_N_KEYS
    assert n_a % SUBLANES == 0 and d == rows
    per_block = ne // rows
    n_blocks = t // tt
    n_tiles = per_block * n_blocks
    tok = lambda n: jnp.minimum(n // per_block, n_blocks - 1)
    tok_prev = lambda n: jnp.maximum(n - 1, 0) // per_block
    kern = functools.partial(_peer_kernel, final_norm=final_norm, tiles_per_block=per_block)
    first = pl.BlockSpec((PEER_HEADS, n_a, tt), lambda n: (0, n % per_block, tok(n)))
    second = pl.BlockSpec((PEER_HEADS, PEER_N_KEYS, tt), lambda n: (0, 0, tok(n)))
    return pl.pallas_call(
        kern,
        grid=(n_tiles + 1,),
        in_specs=[
            pl.BlockSpec((rows, d), lambda n: (0, 0)),
            pl.BlockSpec((d, tt), lambda n: (0, 0)),
            pl.BlockSpec((rows, d), lambda n: ((n + 1) % per_block, 0)),
            pl.BlockSpec((d, tt), lambda n: (0, tok(n + 1))),
            pl.BlockSpec((d, rows), lambda n: (0, (n + per_block - 1) % per_block)),
            first, first, second, second,
            pl.BlockSpec((tt, d), lambda n: (tok_prev(n), 0)),
            pl.BlockSpec((1, d), lambda n: (0, 0)),
        ],
        out_specs=pl.BlockSpec((tt, d), lambda n: (tok_prev(n), 0)),
        out_shape=jax.ShapeDtypeStruct((t, d), F32),
        scratch_shapes=[
            pltpu.VMEM((d, tt), F32),
            pltpu.VMEM((rows, tt), F32),
            pltpu.VMEM((rows, tt), F32),
            pltpu.VMEM((rows, tt), BF16),
            pltpu.VMEM((rows, tt), BF16),
            pltpu.VMEM((2, PEER_HEADS, EW_ROWS, tt), BF16),
        ],
        compiler_params=pltpu.CompilerParams(
            dimension_semantics=("arbitrary",), vmem_limit_bytes=VMEM_LIMIT),
        name="peer",
    )(u, hnt, u, hnt, vt, e1, need, e2, rank, h, g)


def _largest_tile(n, cap):
    t = cap
    while n % t:
        t //= 2
    return t


def kernel(x, norm1_g, w_in, b_f, lambda_q1, lambda_k1, lambda_q2, lambda_k2, subln_g, w_out,
           norm2_g, peer_w_q, peer_sub_keys, peer_u, peer_v, final_g):
    b, s, d = x.shape
    depth = w_in.shape[0]
    t = b * s
    tile_s = _largest_tile(s, 512)
    tile_in = _largest_tile(s, 512)
    tile_route = _largest_tile(t, 256)
    tile_peer = _largest_tile(t, 512)
    rows_peer = 1024

    posq, poskt, slope_tab = _alibi_operands(s)
    h = x
    for layer in range(depth):
        w = w_in[layer]
        o = 0
        parts = {}
        for name, width in (("qa", FOX_WIDTH), ("ka", FOX_WIDTH), ("va", FOX_WIDTH), ("f", FOX_HEADS),
                            ("qb", DIFF_WIDTH), ("kb", DIFF_WIDTH), ("vb", DIFF_WIDTH)):
            parts[name] = w[:, o:o + width]
            o += width
        group_of = [g in GATE_GROUPS for g in range(0, LANES, FOX_HEADS)]
        gate = jnp.concatenate([parts["f"] if on else jnp.zeros((d, FOX_HEADS), F32) for on in group_of], axis=1)
        bf128 = jnp.concatenate([b_f[layer].astype(F32) if on else jnp.zeros((FOX_HEADS,), F32)
                                 for on in group_of])[None, :]
        va_pad = jnp.pad(parts["va"].reshape(d, FOX_HEADS, FOX_HEAD_DIM),
                         ((0, 0), (0, 0), (0, FOX_HEAD_DIM))).reshape(d, 2 * FOX_WIDTH)
        w_all = jnp.concatenate([parts["qa"], va_pad, parts["qb"], parts["vb"], gate], axis=1).astype(BF16)
        w_kt = jnp.concatenate([parts["ka"], parts["kb"]], axis=1).T.astype(BF16)

        qa, bq, kta, bkt, vaa, qb, ktb, vb = _inproj(h, norm1_g[layer][None, :], w_all, w_kt, bf128, tile_in)
        out_a = _fox(qa, bq, kta, bkt, vaa, tile_s)

        lambda_init = 0.8 - 0.6 * math.exp(-0.3 * layer)
        lam_rows = jnp.zeros((4, LANES), F32)
        for r, p in enumerate((lambda_q1, lambda_k1, lambda_q2, lambda_k2)):
            lam_rows = lam_rows.at[r, :DIFF_HALF_DIM].set(p[layer].astype(F32))
        out_b = _diff(qb, posq, ktb, poskt, vb, slope_tab, lam_rows, subln_g[layer][None, :].astype(F32),
                      tile_s, lambda_init)

        keys = peer_sub_keys[layer].reshape(2 * PEER_HEADS, PEER_N_KEYS, PEER_HALF_DIM).astype(BF16)
        h2, hnt, e1, need, e2, rank = _route(
            out_a.reshape(t, FOX_WIDTH), out_b.reshape(t, DIFF_WIDTH), h.reshape(t, d),
            w_out[layer].astype(BF16), norm2_g[layer][None, :], peer_w_q[layer].astype(BF16), keys,
            tile_route)

        last = layer == depth - 1
        h = _peer(peer_u[layer].astype(BF16), hnt, peer_v[layer].T.astype(BF16), e1, need, e2, rank,
                  h2, final_g[None, :], tile_peer, rows_peer, last).reshape(b, s, d)
    return h
```

```python
import functools
import math

import jax
import jax.numpy as jnp
from jax import lax
from jax.experimental import pallas as pl
from jax.experimental.pallas import tpu as pltpu

F32 = jnp.float32
BF16 = jnp.bfloat16

D_MODEL = 1024
FOX_HEADS = 8
FOX_HEAD_DIM = 64
FOX_WIDTH = FOX_HEADS * FOX_HEAD_DIM
DIFF_HEADS = 4
DIFF_HALF_DIM = 64
DIFF_HEAD_DIM = 2 * DIFF_HALF_DIM
DIFF_WIDTH = DIFF_HEADS * DIFF_HEAD_DIM
CHUNK = 64
PEER_HEADS = 8
PEER_N_KEYS = 128
PEER_N_EXPERTS = PEER_N_KEYS * PEER_N_KEYS
PEER_HALF_DIM = 128
PEER_TOPK = 16
RMS_EPS = 1e-6

LANES = 128
SUBLANES = 8
NEG_BIG = -1e30
VMEM_LIMIT = 56 * 1024 * 1024

GATE_GROUPS = (0, 8, 16, 32, 40, 48)

NT_DIMS = (((1,), (1,)), ((), ()))


def _split3(v):
    hi = v.astype(BF16)
    r1 = v - hi.astype(F32)
    mid = r1.astype(BF16)
    lo = (r1 - mid.astype(F32)).astype(BF16)
    return hi, mid, lo


def _inproj_kernel(x_ref, g_ref, w_ref, wkt_ref, bf_ref, tril_ref,
                   qa_ref, bq_ref, kta_ref, bkt_ref, va_ref, qb_ref, ktb_ref, vb_ref, carry_ref):
    tm = x_ref.shape[0]

    @pl.when(pl.program_id(1) == 0)
    def _():
        carry_ref[...] = jnp.zeros_like(carry_ref)

    x = x_ref[...]
    ms = jnp.mean(x * x, axis=-1, keepdims=True)
    xb = (x * lax.rsqrt(ms + RMS_EPS) * g_ref[...]).astype(BF16)
    proj = jnp.dot(xb, w_ref[...], preferred_element_type=F32)
    kt = lax.dot_general(wkt_ref[...], xb, NT_DIMS, preferred_element_type=F32)

    o_va = FOX_WIDTH
    o_qb = o_va + 2 * FOX_WIDTH
    o_vb = o_qb + DIFF_WIDTH
    o_gate = o_vb + DIFF_WIDTH
    qa_ref[...] = (proj[:, 0:FOX_WIDTH] * 0.125).astype(BF16)
    qb_ref[...] = (proj[:, o_qb:o_qb + DIFF_WIDTH] * 0.125).astype(BF16)
    vb_ref[...] = proj[:, o_vb:o_vb + DIFF_WIDTH].astype(BF16)
    kta_ref[...] = kt[0:FOX_WIDTH, :].astype(BF16)
    ktb_ref[...] = kt[FOX_WIDTH:FOX_WIDTH + DIFF_WIDTH, :].astype(BF16)
    va = proj[:, o_va:o_qb]
    vlane = lax.broadcasted_iota(jnp.int32, va.shape, 1)
    va_ref[...] = jnp.where((vlane & FOX_HEAD_DIM) != 0, 1.0, va).astype(BF16)

    z = proj[:, o_gate:o_gate + LANES] + bf_ref[...]
    logf = jnp.minimum(z, 0.0) - jnp.log1p(jnp.exp(-jnp.abs(z)))
    tril = tril_ref[...]
    hi, mid, lo = _split3(logf)
    c = (jnp.dot(tril, hi, preferred_element_type=F32)
         + jnp.dot(tril, mid, preferred_element_type=F32)
         + jnp.dot(tril, lo, preferred_element_type=F32)) + carry_ref[0:1, :]
    carry_ref[...] = jnp.broadcast_to(c[tm - 1:tm, :], carry_ref.shape)

    chi, cmid, clo = _split3(c)
    chi, cmid, clo = chi.astype(F32), cmid.astype(F32), clo.astype(F32)
    lane = lax.broadcasted_iota(jnp.int32, c.shape, 1)
    ones_hi = (lane >= 32) & (lane < 56)
    bq = jnp.where(lane < 8, chi, jnp.where(lane < 16, cmid, jnp.where(lane < 24, clo,
                   jnp.where(ones_hi, 1.0, 0.0))))
    bk = jnp.where(lane < 24, 1.0, jnp.where(lane < 32, 0.0, jnp.where(lane < 40, -chi,
                   jnp.where(lane < 48, -cmid, jnp.where(lane < 56, -clo, 0.0)))))
    bq_ref[...] = bq.astype(BF16)
    bkt_ref[...] = bk.T.astype(BF16)


def _inproj(x, g, w_all, w_kt, bf128, tm):
    b, s, d = x.shape
    ncol = w_all.shape[1]
    nkt = w_kt.shape[0]
    tril = jnp.tril(jnp.ones((tm, tm), F32)).astype(BF16)
    row = lambda bi, si: (bi, si, 0)
    col = lambda bi, si: (bi, 0, si)
    const = lambda bi, si: (0, 0)
    shapes = [
        ((b, s, FOX_WIDTH), pl.BlockSpec((None, tm, FOX_WIDTH), row)),
        ((b, s, LANES), pl.BlockSpec((None, tm, LANES), row)),
        ((b, FOX_WIDTH, s), pl.BlockSpec((None, FOX_WIDTH, tm), col)),
        ((b, LANES, s), pl.BlockSpec((None, LANES, tm), col)),
        ((b, s, 2 * FOX_WIDTH), pl.BlockSpec((None, tm, 2 * FOX_WIDTH), row)),
        ((b, s, DIFF_WIDTH), pl.BlockSpec((None, tm, DIFF_WIDTH), row)),
        ((b, DIFF_WIDTH, s), pl.BlockSpec((None, DIFF_WIDTH, tm), col)),
        ((b, s, DIFF_WIDTH), pl.BlockSpec((None, tm, DIFF_WIDTH), row)),
    ]
    return pl.pallas_call(
        _inproj_kernel,
        grid=(b, s // tm),
        in_specs=[
            pl.BlockSpec((None, tm, d), row),
            pl.BlockSpec((1, d), const),
            pl.BlockSpec((d, ncol), const),
            pl.BlockSpec((nkt, d), const),
            pl.BlockSpec((1, LANES), const),
            pl.BlockSpec((tm, tm), const),
        ],
        out_specs=[spec for _, spec in shapes],
        out_shape=[jax.ShapeDtypeStruct(shape, BF16) for shape, _ in shapes],
        scratch_shapes=[pltpu.VMEM((SUBLANES, LANES), F32)],
        compiler_params=pltpu.CompilerParams(
            dimension_semantics=("arbitrary", "arbitrary"), vmem_limit_bytes=VMEM_LIMIT),
        name="inproj",
    )(x, g, w_all, w_kt, bf128, tril)


def _softmax_update(s, m, acc, v_aug):
    m_new = jnp.maximum(m, jnp.max(s, axis=-1, keepdims=True))
    p = jnp.exp(s - m_new).astype(BF16)
    acc_new = jnp.exp(m - m_new) * acc + jnp.dot(p, v_aug, preferred_element_type=F32)
    return m_new, acc_new


def _fox_kernel(q_ref, bq_ref, kt_ref, bkt_ref, v0_ref, v1_ref, o_ref):
    tq = q_ref.shape[0]
    pair = pl.program_id(1)
    qi = pl.program_id(2)
    lane = lax.broadcasted_iota(jnp.int32, (tq, LANES), 1)
    qf = q_ref[...].astype(F32)
    bqf = bq_ref[...].astype(F32)
    qa = []
    for hh in range(2):
        head = 2 * pair + hh
        qm = jnp.where((lane >= FOX_HEAD_DIM * hh) & (lane < FOX_HEAD_DIM * (hh + 1)), qf, 0.0)
        bm = jnp.where(((lane & 7) == head) & (lane < 56), bqf, 0.0)
        qa.append(jnp.concatenate([qm, bm], axis=1).astype(BF16))
    q_all = jnp.concatenate(qa, axis=0)
    v_refs = (v0_ref, v1_ref)

    def step(j, carry, causal):
        ks = pl.multiple_of(j * tq, tq)
        kt = jnp.concatenate([kt_ref[:, pl.ds(ks, tq)], bkt_ref[:, pl.ds(ks, tq)]], axis=0)
        s_all = jnp.dot(q_all, kt, preferred_element_type=F32)
        out = []
        for hh in range(2):
            s = s_all[hh * tq:(hh + 1) * tq]
            if causal:
                row = lax.broadcasted_iota(jnp.int32, s.shape, 0)
                col = lax.broadcasted_iota(jnp.int32, s.shape, 1)
                s = jnp.where(col <= row, s, -jnp.inf)
            m, acc = carry[hh]
            out.append(_softmax_update(s, m, acc, v_refs[hh][pl.ds(ks, tq), :]))
        return tuple(out)

    init = tuple((jnp.full((tq, 1), NEG_BIG, F32), jnp.zeros((tq, LANES), F32)) for _ in range(2))
    carry = lax.fori_loop(0, qi, lambda j, c: step(j, c, False), init)
    (_, acc0), (_, acc1) = step(qi, carry, True)
    o0 = acc0 / pltpu.roll(acc0, FOX_HEAD_DIM, 1)
    o1 = pltpu.roll(acc1, FOX_HEAD_DIM, 1) / acc1
    o_ref[...] = jnp.where(lane < FOX_HEAD_DIM, o0, o1).astype(BF16)


def _fox(qa, bq, kta, bkt, vaa, tq):
    b, s, _ = qa.shape
    npair = FOX_HEADS // 2
    return pl.pallas_call(
        _fox_kernel,
        grid=(b, npair, s // tq),
        in_specs=[
            pl.BlockSpec((None, tq, LANES), lambda bi, p, i: (bi, i, p)),
            pl.BlockSpec((None, tq, LANES), lambda bi, p, i: (bi, i, 0)),
            pl.BlockSpec((None, LANES, s), lambda bi, p, i: (bi, p, 0)),
            pl.BlockSpec((None, LANES, s), lambda bi, p, i: (bi, 0, 0)),
            pl.BlockSpec((None, s, LANES), lambda bi, p, i: (bi, 0, 2 * p)),
            pl.BlockSpec((None, s, LANES), lambda bi, p, i: (bi, 0, 2 * p + 1)),
        ],
        out_specs=pl.BlockSpec((None, tq, LANES), lambda bi, p, i: (bi, i, p)),
        out_shape=jax.ShapeDtypeStruct((b, s, FOX_WIDTH), BF16),
        compiler_params=pltpu.CompilerParams(
            dimension_semantics=("parallel", "parallel", "arbitrary"), vmem_limit_bytes=VMEM_LIMIT),
        name="fox",
    )(qa, bq, kta, bkt, vaa, vaa)


def _diff_kernel(q_ref, pq_ref, kt_ref, pkt_ref, v_ref, slope_ref, lam_ref, g_ref, o_ref, *, out_scale,
                 lambda_init):
    tq = q_ref.shape[0]
    qi = pl.program_id(2)
    lane = lax.broadcasted_iota(jnp.int32, (tq, LANES), 1)
    qf = q_ref[...].astype(F32)
    pq = pq_ref[...]
    q1 = jnp.where(lane < DIFF_HALF_DIM, qf, 0.0).astype(BF16)
    q2 = jnp.where(lane >= DIFF_HALF_DIM, qf, 0.0).astype(BF16)
    q_all = jnp.concatenate([jnp.concatenate([q1, pq], axis=1),
                             jnp.concatenate([q2, pq], axis=1)], axis=0)
    ones = jnp.ones((tq, LANES), BF16)

    def step(j, carry, diagonal):
        ks = pl.multiple_of(j * tq, tq)
        kt = jnp.concatenate([kt_ref[:, pl.ds(ks, tq)], pkt_ref[:, pl.ds(ks, tq)]], axis=0)
        s_all = jnp.dot(q_all, kt, preferred_element_type=F32)
        v_aug = jnp.concatenate([v_ref[pl.ds(ks, tq), :], ones], axis=1)
        if diagonal:
            row = lax.broadcasted_iota(jnp.int32, (tq, tq), 0)
            col = lax.broadcasted_iota(jnp.int32, (tq, tq), 1)
            fix = jnp.where((col // CHUNK) <= (row // CHUNK),
                            (-2.0 * slope_ref[0:1, 0:1]) * jnp.maximum(col - row, 0).astype(F32), -jnp.inf)
        out = []
        for t in range(2):
            s = s_all[t * tq:(t + 1) * tq]
            if diagonal:
                s = s + fix
            m, acc = carry[t]
            out.append(_softmax_update(s, m, acc, v_aug))
        return tuple(out)

    init = tuple((jnp.full((tq, 1), NEG_BIG, F32), jnp.zeros((tq, 2 * LANES), F32)) for _ in range(2))
    carry = lax.fori_loop(0, qi, lambda j, c: step(j, c, False), init)
    (_, acc1), (_, acc2) = step(qi, carry, True)

    lp = lam_ref[...]
    lam = (jnp.exp(jnp.sum(lp[0:1] * lp[1:2], axis=-1, keepdims=True))
           - jnp.exp(jnp.sum(lp[2:3] * lp[3:4], axis=-1, keepdims=True)) + lambda_init)
    o = acc1[:, :LANES] / acc1[:, LANES:] - lam * (acc2[:, :LANES] / acc2[:, LANES:])
    ms = jnp.mean(o * o, axis=-1, keepdims=True)
    o = o * lax.rsqrt(ms + RMS_EPS) * g_ref[...]
    o_ref[...] = (o * out_scale).astype(BF16)


def _diff(qb, posq, ktb, poskt, vb, slopes, lam_rows, subln_g, tq, lambda_init):
    b, s, _ = qb.shape
    kern = functools.partial(_diff_kernel, out_scale=1.0 - lambda_init, lambda_init=lambda_init)
    return pl.pallas_call(
        kern,
        grid=(b, DIFF_HEADS, s // tq),
        in_specs=[
            pl.BlockSpec((None, tq, LANES), lambda bi, h, i: (bi, i, h)),
            pl.BlockSpec((None, tq, LANES), lambda bi, h, i: (h, i, 0)),
            pl.BlockSpec((None, LANES, s), lambda bi, h, i: (bi, h, 0)),
            pl.BlockSpec((LANES, s), lambda bi, h, i: (0, 0)),
            pl.BlockSpec((None, s, LANES), lambda bi, h, i: (bi, 0, h)),
            pl.BlockSpec((None, SUBLANES, LANES), lambda bi, h, i: (h, 0, 0)),
            pl.BlockSpec((4, LANES), lambda bi, h, i: (0, 0)),
            pl.BlockSpec((1, LANES), lambda bi, h, i: (0, 0)),
        ],
        out_specs=pl.BlockSpec((None, tq, LANES), lambda bi, h, i: (bi, i, h)),
        out_shape=jax.ShapeDtypeStruct((b, s, DIFF_WIDTH), BF16),
        compiler_params=pltpu.CompilerParams(
            dimension_semantics=("parallel", "parallel", "arbitrary"), vmem_limit_bytes=VMEM_LIMIT),
        name="diff",
    )(qb, posq, ktb, poskt, vb, slopes, lam_rows, subln_g)


def _alibi_operands(s):
    pos = jnp.arange(s, dtype=jnp.int32)
    hi = (pos // 64).astype(F32)
    lo = (pos % 64).astype(F32)
    slopes = jnp.asarray([2.0 ** (-8.0 * (i + 1) / DIFF_HEADS) for i in range(DIFF_HEADS)], F32)
    sl = slopes[:, None, None]
    lane = jnp.arange(LANES, dtype=jnp.int32)
    qcols = (-sl * 64.0 * hi[None, :, None], -sl * lo[None, :, None], sl * 64.0, sl)
    zq = jnp.zeros((DIFF_HEADS, s, LANES), F32)
    for i, col in enumerate(qcols):
        zq = jnp.where(lane == i, col, zq)
    kcols = (jnp.ones((1, s), F32), jnp.ones((1, s), F32), hi[None], lo[None])
    zk = jnp.zeros((LANES, s), F32)
    for i, col in enumerate(kcols):
        zk = jnp.where(lane[:, None] == i, col, zk)
    slope_tab = jnp.broadcast_to(slopes[:, None, None], (DIFF_HEADS, SUBLANES, LANES))
    return zq.astype(BF16), zk.astype(BF16), slope_tab


def _cmpx(xs, i, j):
    hi = jnp.maximum(xs[i], xs[j])
    lo = jnp.minimum(xs[i], xs[j])
    xs[i], xs[j] = hi, lo


def _bitonic_merge_desc(xs):
    n = len(xs)
    d = n // 2
    while d >= 1:
        for i in range(n):
            if (i & d) == 0:
                _cmpx(xs, i, i + d)
        d //= 2


def _sort_desc(xs):
    n = len(xs)
    k = 2
    while k <= n:
        d = k // 2
        while d >= 1:
            for i in range(n):
                l = i ^ d
                if l > i:
                    if (i & k) == 0:
                        _cmpx(xs, i, l)
                    else:
                        _cmpx(xs, l, i)
            d //= 2
        k *= 2


def _top16_desc(xs):
    xs = list(xs)
    _sort_desc(xs)
    for shift in (4, 2, 1):
        other = [pltpu.roll(xs[15 - i], shift, 0) for i in range(16)]
        xs = [jnp.maximum(xs[i], other[i]) for i in range(16)]
        _bitonic_merge_desc(xs)
    return xs


def _sublane_sum(x):
    for shift in (4, 2, 1):
        x = x + pltpu.roll(x, shift, 0)
    return x


def _route_kernel(oa_ref, ob_ref, x_ref, wo_ref, g_ref, wq_ref, keys_ref,
                  h_ref, hnt_ref, e1_ref, need_ref, e2_ref, rank_ref, sc_ref):
    tm = x_ref.shape[0]
    mixed = jnp.concatenate([oa_ref[...], ob_ref[...]], axis=1)
    h = x_ref[...] + jnp.dot(mixed, wo_ref[...], preferred_element_type=F32)
    h_ref[...] = h
    ms = jnp.mean(h * h, axis=-1, keepdims=True)
    hn_f = h * lax.rsqrt(ms + RMS_EPS) * g_ref[...]
    hn = hn_f.astype(BF16)
    hnt_ref[...] = hn_f.T.astype(BF16)
    q = jnp.dot(hn, wq_ref[...], preferred_element_type=F32).astype(BF16)
    for hp in range(2 * PEER_HEADS):
        qs = q[:, hp * PEER_HALF_DIM:(hp + 1) * PEER_HALF_DIM]
        sc_ref[hp] = lax.dot_general(keys_ref[hp], qs, NT_DIMS, preferred_element_type=F32)

    sub = lax.broadcasted_iota(jnp.int32, (SUBLANES, tm), 0)
    neg_inf = jnp.full((SUBLANES, tm), -jnp.inf, F32)
    groups = PEER_N_KEYS // SUBLANES

    def spread(vals):
        out = vals[SUBLANES - 1]
        for r in range(SUBLANES - 2, -1, -1):
            out = jnp.where(sub == r, vals[r], out)
        return out

    def head_body(hd, _):
        s1 = [sc_ref[2 * hd, g * SUBLANES:(g + 1) * SUBLANES, :] for g in range(groups)]
        s2 = [sc_ref[2 * hd + 1, g * SUBLANES:(g + 1) * SUBLANES, :] for g in range(groups)]
        a = _top16_desc(s1)
        b = _top16_desc(s2)
        b_lo, b_hi, a_hi = spread(b[:8]), spread(b[8:]), spread(a[8:])
        cands = [a[0] + b_lo, a[0] + b_hi, a[1] + b_lo]
        for i, n in ((2, 5), (3, 4), (4, 3), (5, 2), (6, 2), (7, 2)):
            cands.append(jnp.where(sub < n, a[i] + b_lo, neg_inf))
        cands.append(a_hi + b[0])
        top = _top16_desc(cands + [neg_inf] * (16 - len(cands)))
        tau = top[PEER_TOPK - 1]
        z = jnp.zeros((SUBLANES, tm), F32)
        for c in cands:
            z = z + jnp.where(c >= tau, jnp.exp(c - top[0]), 0.0)
        inv_z = 0.5 / _sublane_sum(z)
        for g0 in range(0, groups, 2):
            needs, ranks, e1s, e2s = [], [], [], []
            for g in (g0, g0 + 1):
                need = jnp.full((SUBLANES, tm), PEER_TOPK + 1.0, F32)
                rank = jnp.zeros((SUBLANES, tm), F32)
                for k in range(PEER_TOPK):
                    need = jnp.where(s1[g] + b[k] >= tau, float(PEER_TOPK - k), need)
                for k in reversed(range(PEER_TOPK)):
                    rank = jnp.where(s2[g] >= b[k], float(PEER_TOPK - k), rank)
                needs.append(need)
                ranks.append(rank)
                e1s.append(jnp.exp(s1[g] - a[0]))
                e2s.append(jnp.exp(s2[g] - b[0]) * inv_z)
            rows = pl.ds(g0 * SUBLANES, 2 * SUBLANES)
            e1_ref[hd, rows, :] = jnp.concatenate(e1s, axis=0)
            need_ref[hd, rows, :] = jnp.concatenate(needs, axis=0)
            e2_ref[hd, rows, :] = jnp.concatenate(e2s, axis=0).astype(BF16)
            rank_ref[hd, rows, :] = jnp.concatenate(ranks, axis=0).astype(BF16)
        return 0

    lax.fori_loop(0, PEER_HEADS, head_body, 0)


def _route(oa, ob, x2, wo, g2, wq, keys, tm):
    t, d = x2.shape
    nq = wq.shape[1]
    row = lambda i: (i, 0)
    const2 = lambda i: (0, 0)
    tok3 = lambda i: (0, 0, i)
    score_spec = pl.BlockSpec((PEER_HEADS, PEER_N_KEYS, tm), tok3)
    score_shape = jax.ShapeDtypeStruct((PEER_HEADS, PEER_N_KEYS, t), F32)
    return pl.pallas_call(
        _route_kernel,
        grid=(t // tm,),
        in_specs=[
            pl.BlockSpec((tm, FOX_WIDTH), row),
            pl.BlockSpec((tm, DIFF_WIDTH), row),
            pl.BlockSpec((tm, d), row),
            pl.BlockSpec((d, d), const2),
            pl.BlockSpec((1, d), const2),
            pl.BlockSpec((d, nq), const2),
            pl.BlockSpec((2 * PEER_HEADS, PEER_N_KEYS, PEER_HALF_DIM), lambda i: (0, 0, 0)),
        ],
        out_specs=[
            pl.BlockSpec((tm, d), row),
            pl.BlockSpec((d, tm), lambda i: (0, i)),
            score_spec, score_spec, score_spec, score_spec,
        ],
        out_shape=[
            jax.ShapeDtypeStruct((t, d), F32),
            jax.ShapeDtypeStruct((d, t), BF16),
            score_shape, score_shape,
            jax.ShapeDtypeStruct(score_shape.shape, BF16),
            jax.ShapeDtypeStruct(score_shape.shape, BF16),
        ],
        scratch_shapes=[pltpu.VMEM((2 * PEER_HEADS, PEER_N_KEYS, tm), F32)],
        compiler_params=pltpu.CompilerParams(
            dimension_semantics=("parallel",), vmem_limit_bytes=VMEM_LIMIT),
        name="route",
    )(oa, ob, x2, wo, g2, wq, keys)


def _gelu2(x):
    return x * (1.0 + lax.erf(x * (1.0 / math.sqrt(2.0))))


MXU_TILE = 256
EW_ROWS = 16


def _peer_kernel(u0_ref, xt0_ref, un_ref, xtn_ref, vtp_ref, e1_ref, need_ref, e2_ref, rank_ref, h_ref, g_ref,
                 o_ref, acc_ref, hta_ref, htb_ref, wa_ref, wb_ref, row_ref, *, final_norm, tiles_per_block):
    n = pl.program_id(0)
    rows = un_ref.shape[0]
    tt = xtn_ref.shape[1]
    n_a = rows // PEER_N_KEYS
    col_tiles = tt // MXU_TILE
    assert (rows // MXU_TILE) * col_tiles == n_a

    @pl.when(n == 0)
    def _():
        acc_ref[...] = jnp.zeros_like(acc_ref)
        wb_ref[...] = jnp.zeros_like(wb_ref)
        hta_ref[...] = jnp.dot(u0_ref[...], xt0_ref[...], preferred_element_type=F32)

    def run(ht_cur, ht_nxt, w_cur, w_prev, stages):
        for al in stages:
            rt = pl.ds((al // col_tiles) * MXU_TILE, MXU_TILE)
            ct = pl.ds((al % col_tiles) * MXU_TILE, MXU_TILE)
            ht_nxt[rt, ct] = jnp.dot(un_ref[rt, :], xtn_ref[:, ct], preferred_element_type=F32)
            acc_ref[rt, ct] += jnp.dot(vtp_ref[rt, :], w_prev[:, ct], preferred_element_type=F32)
            for hd in range(PEER_HEADS):
                row_ref[0, hd] = jnp.broadcast_to(e1_ref[hd, al:al + 1, :], (EW_ROWS, tt)).astype(BF16)
                row_ref[1, hd] = jnp.broadcast_to(need_ref[hd, al:al + 1, :], (EW_ROWS, tt)).astype(BF16)
            for rc in range(0, PEER_N_KEYS // EW_ROWS, 2):
                tiles = [slice((rc + i) * EW_ROWS, (rc + i + 1) * EW_ROWS) for i in range(2)]
                w = [None, None]
                for hd in range(PEER_HEADS):
                    e1t = row_ref[0, hd]
                    needt = row_ref[1, hd]
                    for i, rs in enumerate(tiles):
                        e2t = e2_ref[hd, rs, :]
                        part = e1t * jnp.where(rank_ref[hd, rs, :] >= needt, e2t, jnp.zeros_like(e2t))
                        w[i] = part if w[i] is None else w[i] + part
                for i in range(2):
                    rd = pl.ds(al * PEER_N_KEYS + (rc + i) * EW_ROWS, EW_ROWS)
                    w_cur[rd, :] = _gelu2(ht_cur[rd, :]).astype(BF16) * w[i]

    @pl.when(n % 2 == 0)
    def _():
        run(hta_ref, htb_ref, wa_ref, wb_ref, range(n_a))

    @pl.when(n % 2 == 1)
    def _():
        run(htb_ref, hta_ref, wb_ref, wa_ref, range(n_a))

    @pl.when((n % tiles_per_block == 0) & (n > 0))
    def _():
        o = acc_ref[...].T + h_ref[...]
        if final_norm:
            ms = jnp.mean(o * o, axis=-1, keepdims=True)
            o = o * lax.rsqrt(ms + RMS_EPS) * g_ref[...]
        o_ref[...] = o
        acc_ref[...] = jnp.zeros_like(acc_ref)


def _peer(u, hnt, vt, e1, need, e2, rank, h, g, tt, rows, final_norm):
    d, t = hnt.shape
    ne = u.shape[0]
    n_a = rows // PEER_N_KEYS
    assert n_a % SUBLANES == 0 and d == rows
    per_block = ne // rows
    n_blocks = t // tt
    n_tiles = per_block * n_blocks
    tok = lambda n: jnp.minimum(n // per_block, n_blocks - 1)
    tok_prev = lambda n: jnp.maximum(n - 1, 0) // per_block
    kern = functools.partial(_peer_kernel, final_norm=final_norm, tiles_per_block=per_block)
    first = pl.BlockSpec((PEER_HEADS, n_a, tt), lambda n: (0, n % per_block, tok(n)))
    second = pl.BlockSpec((PEER_HEADS, PEER_N_KEYS, tt), lambda n: (0, 0, tok(n)))
    return pl.pallas_call(
        kern,
        grid=(n_tiles + 1,),
        in_specs=[
            pl.BlockSpec((rows, d), lambda n: (0, 0)),
            pl.BlockSpec((d, tt), lambda n: (0, 0)),
            pl.BlockSpec((rows, d), lambda n: ((n + 1) % per_block, 0)),
            pl.BlockSpec((d, tt), lambda n: (0, tok(n + 1))),
            pl.BlockSpec((d, rows), lambda n: (0, (n + per_block - 1) % per_block)),
            first, first, second, second,
            pl.BlockSpec((tt, d), lambda n: (tok_prev(n), 0)),
            pl.BlockSpec((1, d), lambda n: (0, 0)),
        ],
        out_specs=pl.BlockSpec((tt, d), lambda n: (tok_prev(n), 0)),
        out_shape=jax.ShapeDtypeStruct((t, d), F32),
        scratch_shapes=[
            pltpu.VMEM((d, tt), F32),
            pltpu.VMEM((rows, tt), F32),
            pltpu.VMEM((rows, tt), F32),
            pltpu.VMEM((rows, tt), BF16),
            pltpu.VMEM((rows, tt), BF16),
            pltpu.VMEM((2, PEER_HEADS, EW_ROWS, tt), BF16),
        ],
        compiler_params=pltpu.CompilerParams(
            dimension_semantics=("arbitrary",), vmem_limit_bytes=VMEM_LIMIT),
        name="peer",
    )(u, hnt, u, hnt, vt, e1, need, e2, rank, h, g)


def _largest_tile(n, cap):
    t = cap
    while n % t:
        t //= 2
    return t


def kernel(x, norm1_g, w_in, b_f, lambda_q1, lambda_k1, lambda_q2, lambda_k2, subln_g, w_out,
           norm2_g, peer_w_q, peer_sub_keys, peer_u, peer_v, final_g):
    b, s, d = x.shape
    depth = w_in.shape[0]
    t = b * s
    tile_s = _largest_tile(s, 512)
    tile_in = _largest_tile(s, 512)
    tile_route = _largest_tile(t, 256)
    tile_peer = _largest_tile(t, 512)
    rows_peer = 1024

    posq, poskt, slope_tab = _alibi_operands(s)
    h = x
    for layer in range(depth):
        w = w_in[layer]
        o = 0
        parts = {}
        for name, width in (("qa", FOX_WIDTH), ("ka", FOX_WIDTH), ("va", FOX_WIDTH), ("f", FOX_HEADS),
                            ("qb", DIFF_WIDTH), ("kb", DIFF_WIDTH), ("vb", DIFF_WIDTH)):
            parts[name] = w[:, o:o + width]
            o += width
        group_of = [g in GATE_GROUPS for g in range(0, LANES, FOX_HEADS)]
        gate = jnp.concatenate([parts["f"] if on else jnp.zeros((d, FOX_HEADS), F32) for on in group_of], axis=1)
        bf128 = jnp.concatenate([b_f[layer].astype(F32) if on else jnp.zeros((FOX_HEADS,), F32)
                                 for on in group_of])[None, :]
        va_pad = jnp.pad(parts["va"].reshape(d, FOX_HEADS, FOX_HEAD_DIM),
                         ((0, 0), (0, 0), (0, FOX_HEAD_DIM))).reshape(d, 2 * FOX_WIDTH)
        w_all = jnp.concatenate([parts["qa"], va_pad, parts["qb"], parts["vb"], gate], axis=1).astype(BF16)
        w_kt = jnp.concatenate([parts["ka"], parts["kb"]], axis=1).T.astype(BF16)

        qa, bq, kta, bkt, vaa, qb, ktb, vb = _inproj(h, norm1_g[layer][None, :], w_all, w_kt, bf128, tile_in)
        out_a = _fox(qa, bq, kta, bkt, vaa, tile_s)

        lambda_init = 0.8 - 0.6 * math.exp(-0.3 * layer)
        lam_rows = jnp.zeros((4, LANES), F32)
        for r, p in enumerate((lambda_q1, lambda_k1, lambda_q2, lambda_k2)):
            lam_rows = lam_rows.at[r, :DIFF_HALF_DIM].set(p[layer].astype(F32))
        out_b = _diff(qb, posq, ktb, poskt, vb, slope_tab, lam_rows, subln_g[layer][None, :].astype(F32),
                      tile_s, lambda_init)

        keys = peer_sub_keys[layer].reshape(2 * PEER_HEADS, PEER_N_KEYS, PEER_HALF_DIM).astype(BF16)
        h2, hnt, e1, need, e2, rank = _route(
            out_a.reshape(t, FOX_WIDTH), out_b.reshape(t, DIFF_WIDTH), h.reshape(t, d),
            w_out[layer].astype(BF16), norm2_g[layer][None, :], peer_w_q[layer].astype(BF16), keys,
            tile_route)

        last = layer == depth - 1
        h = _peer(peer_u[layer].astype(BF16), hnt, peer_v[layer].T.astype(BF16), e1, need, e2, rank,
                  h2, final_g[None, :], tile_peer, rows_peer, last).reshape(b, s, d)
    return h
```

```python
import functools
import math

import jax
import jax.numpy as jnp
from jax import lax
from jax.experimental import pallas as pl
from jax.experimental.pallas import tpu as pltpu

F32 = jnp.float32
BF16 = jnp.bfloat16

D_MODEL = 1024
FOX_HEADS = 8
FOX_HEAD_DIM = 64
FOX_WIDTH = FOX_HEADS * FOX_HEAD_DIM
DIFF_HEADS = 4
DIFF_HALF_DIM = 64
DIFF_HEAD_DIM = 2 * DIFF_HALF_DIM
DIFF_WIDTH = DIFF_HEADS * DIFF_HEAD_DIM
CHUNK = 64
PEER_HEADS = 8
PEER_N_KEYS = 128
PEER_N_EXPERTS = PEER_N_KEYS * PEER_N_KEYS
PEER_HALF_DIM = 128
PEER_TOPK = 16
RMS_EPS = 1e-6

LANES = 128
SUBLANES = 8
NEG_BIG = -1e30
VMEM_LIMIT = 56 * 1024 * 1024

GATE_GROUPS = (0, 8, 16, 32, 40, 48)

NT_DIMS = (((1,), (1,)), ((), ()))


def _split3(v):
    hi = v.astype(BF16)
    r1 = v - hi.astype(F32)
    mid = r1.astype(BF16)
    lo = (r1 - mid.astype(F32)).astype(BF16)
    return hi, mid, lo


def _inproj_kernel(x_ref, g_ref, w_ref, wkt_ref, bf_ref, tril_ref,
                   qa_ref, bq_ref, kta_ref, bkt_ref, va_ref, qb_ref, ktb_ref, vb_ref, carry_ref):
    tm = x_ref.shape[0]

    @pl.when(pl.program_id(1) == 0)
    def _():
        carry_ref[...] = jnp.zeros_like(carry_ref)

    x = x_ref[...]
    ms = jnp.mean(x * x, axis=-1, keepdims=True)
    xb = (x * lax.rsqrt(ms + RMS_EPS) * g_ref[...]).astype(BF16)
    proj = jnp.dot(xb, w_ref[...], preferred_element_type=F32)
    kt = lax.dot_general(wkt_ref[...], xb, NT_DIMS, preferred_element_type=F32)

    o_va = FOX_WIDTH
    o_qb = o_va + 2 * FOX_WIDTH
    o_vb = o_qb + DIFF_WIDTH
    o_gate = o_vb + DIFF_WIDTH
    qa_ref[...] = (proj[:, 0:FOX_WIDTH] * 0.125).astype(BF16)
    qb_ref[...] = (proj[:, o_qb:o_qb + DIFF_WIDTH] * 0.125).astype(BF16)
    vb_ref[...] = proj[:, o_vb:o_vb + DIFF_WIDTH].astype(BF16)
    kta_ref[...] = kt[0:FOX_WIDTH, :].astype(BF16)
    ktb_ref[...] = kt[FOX_WIDTH:FOX_WIDTH + DIFF_WIDTH, :].astype(BF16)
    va = proj[:, o_va:o_qb]
    vlane = lax.broadcasted_iota(jnp.int32, va.shape, 1)
    va_ref[...] = jnp.where((vlane & FOX_HEAD_DIM) != 0, 1.0, va).astype(BF16)

    z = proj[:, o_gate:o_gate + LANES] + bf_ref[...]
    logf = jnp.minimum(z, 0.0) - jnp.log1p(jnp.exp(-jnp.abs(z)))
    tril = tril_ref[...]
    hi, mid, lo = _split3(logf)
    c = (jnp.dot(tril, hi, preferred_element_type=F32)
         + jnp.dot(tril, mid, preferred_element_type=F32)
         + jnp.dot(tril, lo, preferred_element_type=F32)) + carry_ref[0:1, :]
    carry_ref[...] = jnp.broadcast_to(c[tm - 1:tm, :], carry_ref.shape)

    chi, cmid, clo = _split3(c)
    chi, cmid, clo = chi.astype(F32), cmid.astype(F32), clo.astype(F32)
    lane = lax.broadcasted_iota(jnp.int32, c.shape, 1)
    ones_hi = (lane >= 32) & (lane < 56)
    bq = jnp.where(lane < 8, chi, jnp.where(lane < 16, cmid, jnp.where(lane < 24, clo,
                   jnp.where(ones_hi, 1.0, 0.0))))
    bk = jnp.where(lane < 24, 1.0, jnp.where(lane < 32, 0.0, jnp.where(lane < 40, -chi,
                   jnp.where(lane < 48, -cmid, jnp.where(lane < 56, -clo, 0.0)))))
    bq_ref[...] = bq.astype(BF16)
    bkt_ref[...] = bk.T.astype(BF16)


def _inproj(x, g, w_all, w_kt, bf128, tm):
    b, s, d = x.shape
    ncol = w_all.shape[1]
    nkt = w_kt.shape[0]
    tril = jnp.tril(jnp.ones((tm, tm), F32)).astype(BF16)
    row = lambda bi, si: (bi, si, 0)
    col = lambda bi, si: (bi, 0, si)
    const = lambda bi, si: (0, 0)
    shapes = [
        ((b, s, FOX_WIDTH), pl.BlockSpec((None, tm, FOX_WIDTH), row)),
        ((b, s, LANES), pl.BlockSpec((None, tm, LANES), row)),
        ((b, FOX_WIDTH, s), pl.BlockSpec((None, FOX_WIDTH, tm), col)),
        ((b, LANES, s), pl.BlockSpec((None, LANES, tm), col)),
        ((b, s, 2 * FOX_WIDTH), pl.BlockSpec((None, tm, 2 * FOX_WIDTH), row)),
        ((b, s, DIFF_WIDTH), pl.BlockSpec((None, tm, DIFF_WIDTH), row)),
        ((b, DIFF_WIDTH, s), pl.BlockSpec((None, DIFF_WIDTH, tm), col)),
        ((b, s, DIFF_WIDTH), pl.BlockSpec((None, tm, DIFF_WIDTH), row)),
    ]
    return pl.pallas_call(
        _inproj_kernel,
        grid=(b, s // tm),
        in_specs=[
            pl.BlockSpec((None, tm, d), row),
            pl.BlockSpec((1, d), const),
            pl.BlockSpec((d, ncol), const),
            pl.BlockSpec((nkt, d), const),
            pl.BlockSpec((1, LANES), const),
            pl.BlockSpec((tm, tm), const),
        ],
        out_specs=[spec for _, spec in shapes],
        out_shape=[jax.ShapeDtypeStruct(shape, BF16) for shape, _ in shapes],
        scratch_shapes=[pltpu.VMEM((SUBLANES, LANES), F32)],
        compiler_params=pltpu.CompilerParams(
            dimension_semantics=("arbitrary", "arbitrary"), vmem_limit_bytes=VMEM_LIMIT),
        name="inproj",
    )(x, g, w_all, w_kt, bf128, tril)


def _softmax_update(s, m, acc, v_aug):
    m_new = jnp.maximum(m, jnp.max(s, axis=-1, keepdims=True))
    p = jnp.exp(s - m_new).astype(BF16)
    acc_new = jnp.exp(m - m_new) * acc + jnp.dot(p, v_aug, preferred_element_type=F32)
    return m_new, acc_new


def _fox_kernel(q_ref, bq_ref, kt_ref, bkt_ref, v0_ref, v1_ref, o_ref):
    tq = q_ref.shape[0]
    pair = pl.program_id(1)
    qi = pl.program_id(2)
    lane = lax.broadcasted_iota(jnp.int32, (tq, LANES), 1)
    qf = q_ref[...].astype(F32)
    bqf = bq_ref[...].astype(F32)
    qa = []
    for hh in range(2):
        head = 2 * pair + hh
        qm = jnp.where((lane >= FOX_HEAD_DIM * hh) & (lane < FOX_HEAD_DIM * (hh + 1)), qf, 0.0)
        bm = jnp.where(((lane & 7) == head) & (lane < 56), bqf, 0.0)
        qa.append(jnp.concatenate([qm, bm], axis=1).astype(BF16))
    q_all = jnp.concatenate(qa, axis=0)
    v_refs = (v0_ref, v1_ref)

    def step(j, carry, causal):
        ks = pl.multiple_of(j * tq, tq)
        kt = jnp.concatenate([kt_ref[:, pl.ds(ks, tq)], bkt_ref[:, pl.ds(ks, tq)]], axis=0)
        s_all = jnp.dot(q_all, kt, preferred_element_type=F32)
        out = []
        for hh in range(2):
            s = s_all[hh * tq:(hh + 1) * tq]
            if causal:
                row = lax.broadcasted_iota(jnp.int32, s.shape, 0)
                col = lax.broadcasted_iota(jnp.int32, s.shape, 1)
                s = jnp.where(col <= row, s, -jnp.inf)
            m, acc = carry[hh]
            out.append(_softmax_update(s, m, acc, v_refs[hh][pl.ds(ks, tq), :]))
        return tuple(out)

    init = tuple((jnp.full((tq, 1), NEG_BIG, F32), jnp.zeros((tq, LANES), F32)) for _ in range(2))
    def quad(j, c):
        for t in range(4):
            c = step(4 * j + t, c, False)
        return c

    carry = lax.fori_loop(0, qi // 4, quad, init)
    base = (qi // 4) * 4
    carry = lax.cond(qi % 4 >= 2, lambda c: step(base + 1, step(base, c, False), False), lambda c: c, carry)
    (_, acc0), (_, acc1) = lax.cond(qi % 2 == 1, lambda c: step(qi, step(qi - 1, c, False), True),
                                    lambda c: step(qi, c, True), carry)
    o0 = acc0 / pltpu.roll(acc0, FOX_HEAD_DIM, 1)
    o1 = pltpu.roll(acc1, FOX_HEAD_DIM, 1) / acc1
    o_ref[...] = jnp.where(lane < FOX_HEAD_DIM, o0, o1).astype(BF16)


def _fox(qa, bq, kta, bkt, vaa, tq):
    b, s, _ = qa.shape
    npair = FOX_HEADS // 2
    return pl.pallas_call(
        _fox_kernel,
        grid=(b, npair, s // tq),
        in_specs=[
            pl.BlockSpec((None, tq, LANES), lambda bi, p, i: (bi, i, p)),
            pl.BlockSpec((None, tq, LANES), lambda bi, p, i: (bi, i, 0)),
            pl.BlockSpec((None, LANES, s), lambda bi, p, i: (bi, p, 0)),
            pl.BlockSpec((None, LANES, s), lambda bi, p, i: (bi, 0, 0)),
            pl.BlockSpec((None, s, LANES), lambda bi, p, i: (bi, 0, 2 * p)),
            pl.BlockSpec((None, s, LANES), lambda bi, p, i: (bi, 0, 2 * p + 1)),
        ],
        out_specs=pl.BlockSpec((None, tq, LANES), lambda bi, p, i: (bi, i, p)),
        out_shape=jax.ShapeDtypeStruct((b, s, FOX_WIDTH), BF16),
        compiler_params=pltpu.CompilerParams(
            dimension_semantics=("parallel", "parallel", "arbitrary"), vmem_limit_bytes=VMEM_LIMIT),
        name="fox",
    )(qa, bq, kta, bkt, vaa, vaa)


def _diff_kernel(q_ref, pq_ref, kt_ref, pkt_ref, v_ref, slope_ref, lam_ref, g_ref, o_ref, *, out_scale,
                 lambda_init):
    tq = q_ref.shape[0]
    qi = pl.program_id(2)
    lane = lax.broadcasted_iota(jnp.int32, (tq, LANES), 1)
    qf = q_ref[...].astype(F32)
    pq = pq_ref[...]
    q1 = jnp.where(lane < DIFF_HALF_DIM, qf, 0.0).astype(BF16)
    q2 = jnp.where(lane >= DIFF_HALF_DIM, qf, 0.0).astype(BF16)
    q_all = jnp.concatenate([jnp.concatenate([q1, pq], axis=1),
                             jnp.concatenate([q2, pq], axis=1)], axis=0)
    ones = jnp.ones((tq, LANES), BF16)

    def step(j, carry, diagonal):
        ks = pl.multiple_of(j * tq, tq)
        kt = jnp.concatenate([kt_ref[:, pl.ds(ks, tq)], pkt_ref[:, pl.ds(ks, tq)]], axis=0)
        s_all = jnp.dot(q_all, kt, preferred_element_type=F32)
        v_aug = jnp.concatenate([v_ref[pl.ds(ks, tq), :], ones], axis=1)
        if diagonal:
            row = lax.broadcasted_iota(jnp.int32, (tq, tq), 0)
            col = lax.broadcasted_iota(jnp.int32, (tq, tq), 1)
            fix = jnp.where((col // CHUNK) <= (row // CHUNK),
                            (-2.0 * slope_ref[0:1, 0:1]) * jnp.maximum(col - row, 0).astype(F32), -jnp.inf)
        out = []
        for t in range(2):
            s = s_all[t * tq:(t + 1) * tq]
            if diagonal:
                s = s + fix
            m, acc = carry[t]
            out.append(_softmax_update(s, m, acc, v_aug))
        return tuple(out)

    init = tuple((jnp.full((tq, 1), NEG_BIG, F32), jnp.zeros((tq, 2 * LANES), F32)) for _ in range(2))
    carry = lax.fori_loop(0, qi // 2, lambda j, c: step(2 * j + 1, step(2 * j, c, False), False), init)
    (_, acc1), (_, acc2) = lax.cond(qi % 2 == 1, lambda c: step(qi, step(qi - 1, c, False), True),
                                    lambda c: step(qi, c, True), carry)

    lp = lam_ref[...]
    lam = (jnp.exp(jnp.sum(lp[0:1] * lp[1:2], axis=-1, keepdims=True))
           - jnp.exp(jnp.sum(lp[2:3] * lp[3:4], axis=-1, keepdims=True)) + lambda_init)
    o = acc1[:, :LANES] / acc1[:, LANES:] - lam * (acc2[:, :LANES] / acc2[:, LANES:])
    ms = jnp.mean(o * o, axis=-1, keepdims=True)
    o = o * lax.rsqrt(ms + RMS_EPS) * g_ref[...]
    o_ref[...] = (o * out_scale).astype(BF16)


def _diff(qb, posq, ktb, poskt, vb, slopes, lam_rows, subln_g, tq, lambda_init):
    b, s, _ = qb.shape
    kern = functools.partial(_diff_kernel, out_scale=1.0 - lambda_init, lambda_init=lambda_init)
    return pl.pallas_call(
        kern,
        grid=(b, DIFF_HEADS, s // tq),
        in_specs=[
            pl.BlockSpec((None, tq, LANES), lambda bi, h, i: (bi, i, h)),
            pl.BlockSpec((None, tq, LANES), lambda bi, h, i: (h, i, 0)),
            pl.BlockSpec((None, LANES, s), lambda bi, h, i: (bi, h, 0)),
            pl.BlockSpec((LANES, s), lambda bi, h, i: (0, 0)),
            pl.BlockSpec((None, s, LANES), lambda bi, h, i: (bi, 0, h)),
            pl.BlockSpec((None, SUBLANES, LANES), lambda bi, h, i: (h, 0, 0)),
            pl.BlockSpec((4, LANES), lambda bi, h, i: (0, 0)),
            pl.BlockSpec((1, LANES), lambda bi, h, i: (0, 0)),
        ],
        out_specs=pl.BlockSpec((None, tq, LANES), lambda bi, h, i: (bi, i, h)),
        out_shape=jax.ShapeDtypeStruct((b, s, DIFF_WIDTH), BF16),
        compiler_params=pltpu.CompilerParams(
            dimension_semantics=("parallel", "parallel", "arbitrary"), vmem_limit_bytes=VMEM_LIMIT),
        name="diff",
    )(qb, posq, ktb, poskt, vb, slopes, lam_rows, subln_g)


def _alibi_operands(s):
    pos = jnp.arange(s, dtype=jnp.int32)
    hi = (pos // 64).astype(F32)
    lo = (pos % 64).astype(F32)
    slopes = jnp.asarray([2.0 ** (-8.0 * (i + 1) / DIFF_HEADS) for i in range(DIFF_HEADS)], F32)
    sl = slopes[:, None, None]
    lane = jnp.arange(LANES, dtype=jnp.int32)
    qcols = (-sl * 64.0 * hi[None, :, None], -sl * lo[None, :, None], sl * 64.0, sl)
    zq = jnp.zeros((DIFF_HEADS, s, LANES), F32)
    for i, col in enumerate(qcols):
        zq = jnp.where(lane == i, col, zq)
    kcols = (jnp.ones((1, s), F32), jnp.ones((1, s), F32), hi[None], lo[None])
    zk = jnp.zeros((LANES, s), F32)
    for i, col in enumerate(kcols):
        zk = jnp.where(lane[:, None] == i, col, zk)
    slope_tab = jnp.broadcast_to(slopes[:, None, None], (DIFF_HEADS, SUBLANES, LANES))
    return zq.astype(BF16), zk.astype(BF16), slope_tab


def _cmpx(xs, i, j):
    hi = jnp.maximum(xs[i], xs[j])
    lo = jnp.minimum(xs[i], xs[j])
    xs[i], xs[j] = hi, lo


def _bitonic_merge_desc(xs):
    n = len(xs)
    d = n // 2
    while d >= 1:
        for i in range(n):
            if (i & d) == 0:
                _cmpx(xs, i, i + d)
        d //= 2


def _sort_desc(xs):
    n = len(xs)
    k = 2
    while k <= n:
        d = k // 2
        while d >= 1:
            for i in range(n):
                l = i ^ d
                if l > i:
                    if (i & k) == 0:
                        _cmpx(xs, i, l)
                    else:
                        _cmpx(xs, l, i)
            d //= 2
        k *= 2


def _top16_desc(xs):
    xs = list(xs)
    _sort_desc(xs)
    for shift in (4, 2, 1):
        other = [pltpu.roll(xs[15 - i], shift, 0) for i in range(16)]
        xs = [jnp.maximum(xs[i], other[i]) for i in range(16)]
        _bitonic_merge_desc(xs)
    return xs


def _sublane_sum(x):
    for shift in (4, 2, 1):
        x = x + pltpu.roll(x, shift, 0)
    return x


def _route_kernel(oa_ref, ob_ref, x_ref, wo_ref, g_ref, wq_ref, keys_ref,
                  h_ref, hnt_ref, e1_ref, need_ref, e2_ref, rank_ref, sc_ref):
    tm = x_ref.shape[0]
    mixed = jnp.concatenate([oa_ref[...], ob_ref[...]], axis=1)
    h = x_ref[...] + jnp.dot(mixed, wo_ref[...], preferred_element_type=F32)
    h_ref[...] = h
    ms = jnp.mean(h * h, axis=-1, keepdims=True)
    hn_f = h * lax.rsqrt(ms + RMS_EPS) * g_ref[...]
    hn = hn_f.astype(BF16)
    hnt_ref[...] = hn_f.T.astype(BF16)
    q = jnp.dot(hn, wq_ref[...], preferred_element_type=F32).astype(BF16)
    for hp in range(2 * PEER_HEADS):
        qs = q[:, hp * PEER_HALF_DIM:(hp + 1) * PEER_HALF_DIM]
        sc_ref[hp] = lax.dot_general(keys_ref[hp], qs, NT_DIMS, preferred_element_type=F32)

    sub = lax.broadcasted_iota(jnp.int32, (SUBLANES, tm), 0)
    neg_inf = jnp.full((SUBLANES, tm), -jnp.inf, F32)
    groups = PEER_N_KEYS // SUBLANES

    def spread(vals):
        out = vals[SUBLANES - 1]
        for r in range(SUBLANES - 2, -1, -1):
            out = jnp.where(sub == r, vals[r], out)
        return out

    def head_body(hd, _):
        s1 = [sc_ref[2 * hd, g * SUBLANES:(g + 1) * SUBLANES, :] for g in range(groups)]
        s2 = [sc_ref[2 * hd + 1, g * SUBLANES:(g + 1) * SUBLANES, :] for g in range(groups)]
        a = _top16_desc(s1)
        b = _top16_desc(s2)
        b_lo, b_hi, a_hi = spread(b[:8]), spread(b[8:]), spread(a[8:])
        cands = [a[0] + b_lo, a[0] + b_hi, a[1] + b_lo]
        for i, n in ((2, 5), (3, 4), (4, 3), (5, 2), (6, 2), (7, 2)):
            cands.append(jnp.where(sub < n, a[i] + b_lo, neg_inf))
        cands.append(a_hi + b[0])
        top = _top16_desc(cands + [neg_inf] * (16 - len(cands)))
        tau = top[PEER_TOPK - 1]
        z = jnp.zeros((SUBLANES, tm), F32)
        for c in cands:
            z = z + jnp.where(c >= tau, jnp.exp(c - top[0]), 0.0)
        inv_z = 0.5 / _sublane_sum(z)
        for g0 in range(0, groups, 2):
            needs, ranks, e1s, e2s = [], [], [], []
            for g in (g0, g0 + 1):
                need = jnp.full((SUBLANES, tm), PEER_TOPK + 1.0, F32)
                rank = jnp.zeros((SUBLANES, tm), F32)
                for k in range(PEER_TOPK):
                    need = jnp.where(s1[g] + b[k] >= tau, float(PEER_TOPK - k), need)
                for k in reversed(range(PEER_TOPK)):
                    rank = jnp.where(s2[g] >= b[k], float(PEER_TOPK - k), rank)
                needs.append(need)
                ranks.append(rank)
                e1s.append(jnp.exp(s1[g] - a[0]))
                e2s.append(jnp.exp(s2[g] - b[0]) * inv_z)
            rows = pl.ds(g0 * SUBLANES, 2 * SUBLANES)
            e1_ref[hd, rows, :] = jnp.concatenate(e1s, axis=0)
            need_ref[hd, rows, :] = jnp.concatenate(needs, axis=0)
            e2_ref[hd, rows, :] = jnp.concatenate(e2s, axis=0).astype(BF16)
            rank_ref[hd, rows, :] = jnp.concatenate(ranks, axis=0).astype(BF16)
        return 0

    lax.fori_loop(0, PEER_HEADS, head_body, 0)


def _route(oa, ob, x2, wo, g2, wq, keys, tm):
    t, d = x2.shape
    nq = wq.shape[1]
    row = lambda i: (i, 0)
    const2 = lambda i: (0, 0)
    tok3 = lambda i: (0, 0, i)
    score_spec = pl.BlockSpec((PEER_HEADS, PEER_N_KEYS, tm), tok3)
    score_shape = jax.ShapeDtypeStruct((PEER_HEADS, PEER_N_KEYS, t), F32)
    return pl.pallas_call(
        _route_kernel,
        grid=(t // tm,),
        in_specs=[
            pl.BlockSpec((tm, FOX_WIDTH), row),
            pl.BlockSpec((tm, DIFF_WIDTH), row),
            pl.BlockSpec((tm, d), row),
            pl.BlockSpec((d, d), const2),
            pl.BlockSpec((1, d), const2),
            pl.BlockSpec((d, nq), const2),
            pl.BlockSpec((2 * PEER_HEADS, PEER_N_KEYS, PEER_HALF_DIM), lambda i: (0, 0, 0)),
        ],
        out_specs=[
            pl.BlockSpec((tm, d), row),
            pl.BlockSpec((d, tm), lambda i: (0, i)),
            score_spec, score_spec, score_spec, score_spec,
        ],
        out_shape=[
            jax.ShapeDtypeStruct((t, d), F32),
            jax.ShapeDtypeStruct((d, t), BF16),
            score_shape, score_shape,
            jax.ShapeDtypeStruct(score_shape.shape, BF16),
            jax.ShapeDtypeStruct(score_shape.shape, BF16),
        ],
        scratch_shapes=[pltpu.VMEM((2 * PEER_HEADS, PEER_N_KEYS, tm), F32)],
        compiler_params=pltpu.CompilerParams(
            dimension_semantics=("parallel",), vmem_limit_bytes=VMEM_LIMIT),
        name="route",
    )(oa, ob, x2, wo, g2, wq, keys)


def _gelu2(x):
    return x * (1.0 + lax.erf(x * (1.0 / math.sqrt(2.0))))


MXU_TILE = 256
EW_ROWS = 16
STRIP_TILES = 4


def _peer_kernel(u0_ref, xt0_ref, un_ref, xtn_ref, vtp_ref, e1_ref, need_ref, e2_ref, rank_ref, h_ref, g_ref,
                 o_ref, acc_ref, ht_ref, w_ref, row_ref, *, final_norm, tiles_per_block):
    n = pl.program_id(0)
    rows = un_ref.shape[0]
    tt = xtn_ref.shape[1]
    n_a = rows // PEER_N_KEYS
    col_tiles = tt // MXU_TILE
    assert (rows // MXU_TILE) * col_tiles == n_a

    @pl.when(n == 0)
    def _():
        acc_ref[...] = jnp.zeros_like(acc_ref)
        w_ref[1] = jnp.zeros(w_ref.shape[1:], w_ref.dtype)
        ht_ref[0] = jnp.dot(u0_ref[...], xt0_ref[...], preferred_element_type=F32)

    def run(ht_cur, ht_nxt, w_cur, w_prev, stages):
        for al in stages:
            rt = pl.ds((al // col_tiles) * MXU_TILE, MXU_TILE)
            ct = pl.ds((al % col_tiles) * MXU_TILE, MXU_TILE)
            ht_nxt[rt, ct] = jnp.dot(un_ref[rt, :], xtn_ref[:, ct], preferred_element_type=F32)
            acc_ref[rt, ct] += jnp.dot(vtp_ref[rt, :], w_prev[:, ct], preferred_element_type=F32)
            for hd in range(PEER_HEADS):
                row_ref[0, hd] = jnp.broadcast_to(e1_ref[hd, al:al + 1, :], (EW_ROWS, tt)).astype(BF16)
                row_ref[1, hd] = jnp.broadcast_to(need_ref[hd, al:al + 1, :], (EW_ROWS, tt)).astype(BF16)
            for rc in range(0, PEER_N_KEYS // EW_ROWS, STRIP_TILES):
                tiles = [slice((rc + i) * EW_ROWS, (rc + i + 1) * EW_ROWS) for i in range(STRIP_TILES)]
                w = [None] * STRIP_TILES
                for hd in range(PEER_HEADS):
                    e1t = row_ref[0, hd]
                    needt = row_ref[1, hd]
                    for i, rs in enumerate(tiles):
                        e2t = e2_ref[hd, rs, :]
                        part = e1t * jnp.where(rank_ref[hd, rs, :] >= needt, e2t, jnp.zeros_like(e2t))
                        w[i] = part if w[i] is None else w[i] + part
                for i in range(STRIP_TILES):
                    rd = pl.ds(al * PEER_N_KEYS + (rc + i) * EW_ROWS, EW_ROWS)
                    w_cur[rd, :] = _gelu2(ht_cur[rd, :]).astype(BF16) * w[i]

    cur = n % 2
    run(ht_ref.at[cur], ht_ref.at[1 - cur], w_ref.at[cur], w_ref.at[1 - cur], range(n_a))

    @pl.when((n % tiles_per_block == 0) & (n > 0))
    def _():
        o = acc_ref[...].T + h_ref[...]
        if final_norm:
            ms = jnp.mean(o * o, axis=-1, keepdims=True)
            o = o * lax.rsqrt(ms + RMS_EPS) * g_ref[...]
        o_ref[...] = o
        acc_ref[...] = jnp.zeros_like(acc_ref)


def _peer(u, hnt, vt, e1, need, e2, rank, h, g, tt, rows, final_norm):
    d, t = hnt.shape
    ne = u.shape[0]
    n_a = rows // PEER_N_KEYS
    assert n_a % SUBLANES == 0 and d == rows
    per_block = ne // rows
    n_blocks = t // tt
    n_tiles = per_block * n_blocks
    tok = lambda n: jnp.minimum(n // per_block, n_blocks - 1)
    tok_prev = lambda n: jnp.maximum(n - 1, 0) // per_block
    kern = functools.partial(_peer_kernel, final_norm=final_norm, tiles_per_block=per_block)
    first = pl.BlockSpec((PEER_HEADS, n_a, tt), lambda n: (0, n % per_block, tok(n)))
    second = pl.BlockSpec((PEER_HEADS, PEER_N_KEYS, tt), lambda n: (0, 0, tok(n)))
    return pl.pallas_call(
        kern,
        grid=(n_tiles + 1,),
        in_specs=[
            pl.BlockSpec((rows, d), lambda n: (0, 0)),
            pl.BlockSpec((d, tt), lambda n: (0, 0)),
            pl.BlockSpec((rows, d), lambda n: ((n + 1) % per_block, 0)),
            pl.BlockSpec((d, tt), lambda n: (0, tok(n + 1))),
            pl.BlockSpec((d, rows), lambda n: (0, (n + per_block - 1) % per_block)),
            first, first, second, second,
            pl.BlockSpec((tt, d), lambda n: (tok_prev(n), 0)),
            pl.BlockSpec((1, d), lambda n: (0, 0)),
        ],
        out_specs=pl.BlockSpec((tt, d), lambda n: (tok_prev(n), 0)),
        out_shape=jax.ShapeDtypeStruct((t, d), F32),
        scratch_shapes=[
            pltpu.VMEM((d, tt), F32),
            pltpu.VMEM((2, rows, tt), F32),
            pltpu.VMEM((2, rows, tt), BF16),
            pltpu.VMEM((2, PEER_HEADS, EW_ROWS, tt), BF16),
        ],
        compiler_params=pltpu.CompilerParams(
            dimension_semantics=("arbitrary",), vmem_limit_bytes=VMEM_LIMIT),
        name="peer",
    )(u, hnt, u, hnt, vt, e1, need, e2, rank, h, g)


def _largest_tile(n, cap):
    t = cap
    while n % t:
        t //= 2
    return t


def kernel(x, norm1_g, w_in, b_f, lambda_q1, lambda_k1, lambda_q2, lambda_k2, subln_g, w_out,
           norm2_g, peer_w_q, peer_sub_keys, peer_u, peer_v, final_g):
    b, s, d = x.shape
    depth = w_in.shape[0]
    t = b * s
    tile_s = _largest_tile(s, 512)
    tile_in = _largest_tile(s, 512)
    tile_route = _largest_tile(t, 256)
    tile_peer = _largest_tile(t, 512)
    rows_peer = 1024

    posq, poskt, slope_tab = _alibi_operands(s)
    h = x
    for layer in range(depth):
        w = w_in[layer]
        o = 0
        parts = {}
        for name, width in (("qa", FOX_WIDTH), ("ka", FOX_WIDTH), ("va", FOX_WIDTH), ("f", FOX_HEADS),
                            ("qb", DIFF_WIDTH), ("kb", DIFF_WIDTH), ("vb", DIFF_WIDTH)):
            parts[name] = w[:, o:o + width]
            o += width
        group_of = [g in GATE_GROUPS for g in range(0, LANES, FOX_HEADS)]
        gate = jnp.concatenate([parts["f"] if on else jnp.zeros((d, FOX_HEADS), F32) for on in group_of], axis=1)
        bf128 = jnp.concatenate([b_f[layer].astype(F32) if on else jnp.zeros((FOX_HEADS,), F32)
                                 for on in group_of])[None, :]
        va_pad = jnp.pad(parts["va"].reshape(d, FOX_HEADS, FOX_HEAD_DIM),
                         ((0, 0), (0, 0), (0, FOX_HEAD_DIM))).reshape(d, 2 * FOX_WIDTH)
        w_all = jnp.concatenate([parts["qa"], va_pad, parts["qb"], parts["vb"], gate], axis=1).astype(BF16)
        w_kt = jnp.concatenate([parts["ka"], parts["kb"]], axis=1).T.astype(BF16)

        qa, bq, kta, bkt, vaa, qb, ktb, vb = _inproj(h, norm1_g[layer][None, :], w_all, w_kt, bf128, tile_in)
        out_a = _fox(qa, bq, kta, bkt, vaa, tile_s)

        lambda_init = 0.8 - 0.6 * math.exp(-0.3 * layer)
        lam_rows = jnp.zeros((4, LANES), F32)
        for r, p in enumerate((lambda_q1, lambda_k1, lambda_q2, lambda_k2)):
            lam_rows = lam_rows.at[r, :DIFF_HALF_DIM].set(p[layer].astype(F32))
        out_b = _diff(qb, posq, ktb, poskt, vb, slope_tab, lam_rows, subln_g[layer][None, :].astype(F32),
                      tile_s, lambda_init)

        keys = peer_sub_keys[layer].reshape(2 * PEER_HEADS, PEER_N_KEYS, PEER_HALF_DIM).astype(BF16)
        h2, hnt, e1, need, e2, rank = _route(
            out_a.reshape(t, FOX_WIDTH), out_b.reshape(t, DIFF_WIDTH), h.reshape(t, d),
            w_out[layer].astype(BF16), norm2_g[layer][None, :], peer_w_q[layer].astype(BF16), keys,
            tile_route)

        last = layer == depth - 1
        h = _peer(peer_u[layer].astype(BF16), hnt, peer_v[layer].T.astype(BF16), e1, need, e2, rank,
                  h2, final_g[None, :], tile_peer, rows_peer, last).reshape(b, s, d)
    return h
```

```python
import functools
import math

import jax
import jax.numpy as jnp
from jax import lax
from jax.experimental import pallas as pl
from jax.experimental.pallas import tpu as pltpu

F32 = jnp.float32
BF16 = jnp.bfloat16

D_MODEL = 1024
FOX_HEADS = 8
FOX_HEAD_DIM = 64
FOX_WIDTH = FOX_HEADS * FOX_HEAD_DIM
DIFF_HEADS = 4
DIFF_HALF_DIM = 64
DIFF_HEAD_DIM = 2 * DIFF_HALF_DIM
DIFF_WIDTH = DIFF_HEADS * DIFF_HEAD_DIM
CHUNK = 64
PEER_HEADS = 8
PEER_N_KEYS = 128
PEER_N_EXPERTS = PEER_N_KEYS * PEER_N_KEYS
PEER_HALF_DIM = 128
PEER_TOPK = 16
RMS_EPS = 1e-6

LANES = 128
SUBLANES = 8
NEG_BIG = -1e30
VMEM_LIMIT = 56 * 1024 * 1024

GATE_GROUPS = (0, 8, 16, 32, 40, 48)

NT_DIMS = (((1,), (1,)), ((), ()))


def _split3(v):
    hi = v.astype(BF16)
    r1 = v - hi.astype(F32)
    mid = r1.astype(BF16)
    lo = (r1 - mid.astype(F32)).astype(BF16)
    return hi, mid, lo


def _inproj_kernel(x_ref, g_ref, w_ref, wkt_ref, bf_ref, tril_ref,
                   qa_ref, bq_ref, kta_ref, bkt_ref, va_ref, qb_ref, ktb_ref, vb_ref, carry_ref):
    tm = x_ref.shape[0]

    @pl.when(pl.program_id(1) == 0)
    def _():
        carry_ref[...] = jnp.zeros_like(carry_ref)

    x = x_ref[...]
    ms = jnp.mean(x * x, axis=-1, keepdims=True)
    xb = (x * lax.rsqrt(ms + RMS_EPS) * g_ref[...]).astype(BF16)
    proj = jnp.dot(xb, w_ref[...], preferred_element_type=F32)
    kt = lax.dot_general(wkt_ref[...], xb, NT_DIMS, preferred_element_type=F32)

    o_va = FOX_WIDTH
    o_qb = o_va + 2 * FOX_WIDTH
    o_vb = o_qb + DIFF_WIDTH
    o_gate = o_vb + DIFF_WIDTH
    qa_ref[...] = (proj[:, 0:FOX_WIDTH] * 0.125).astype(BF16)
    qb_ref[...] = (proj[:, o_qb:o_qb + DIFF_WIDTH] * 0.125).astype(BF16)
    vb_ref[...] = proj[:, o_vb:o_vb + DIFF_WIDTH].astype(BF16)
    kta_ref[...] = kt[0:FOX_WIDTH, :].astype(BF16)
    ktb_ref[...] = kt[FOX_WIDTH:FOX_WIDTH + DIFF_WIDTH, :].astype(BF16)
    va = proj[:, o_va:o_qb]
    vlane = lax.broadcasted_iota(jnp.int32, va.shape, 1)
    va_ref[...] = jnp.where((vlane & FOX_HEAD_DIM) != 0, 1.0, va).astype(BF16)

    z = proj[:, o_gate:o_gate + LANES] + bf_ref[...]
    logf = jnp.minimum(z, 0.0) - jnp.log1p(jnp.exp(-jnp.abs(z)))
    tril = tril_ref[...]
    hi, mid, lo = _split3(logf)
    c = (jnp.dot(tril, hi, preferred_element_type=F32)
         + jnp.dot(tril, mid, preferred_element_type=F32)
         + jnp.dot(tril, lo, preferred_element_type=F32)) + carry_ref[0:1, :]
    carry_ref[...] = jnp.broadcast_to(c[tm - 1:tm, :], carry_ref.shape)

    chi, cmid, clo = _split3(c)
    chi, cmid, clo = chi.astype(F32), cmid.astype(F32), clo.astype(F32)
    lane = lax.broadcasted_iota(jnp.int32, c.shape, 1)
    ones_hi = (lane >= 32) & (lane < 56)
    bq = jnp.where(lane < 8, chi, jnp.where(lane < 16, cmid, jnp.where(lane < 24, clo,
                   jnp.where(ones_hi, 1.0, 0.0))))
    bk = jnp.where(lane < 24, 1.0, jnp.where(lane < 32, 0.0, jnp.where(lane < 40, -chi,
                   jnp.where(lane < 48, -cmid, jnp.where(lane < 56, -clo, 0.0)))))
    bq_ref[...] = bq.astype(BF16)
    bkt_ref[...] = bk.T.astype(BF16)


def _inproj(x, g, w_all, w_kt, bf128, tm):
    b, s, d = x.shape
    ncol = w_all.shape[1]
    nkt = w_kt.shape[0]
    tril = jnp.tril(jnp.ones((tm, tm), F32)).astype(BF16)
    row = lambda bi, si: (bi, si, 0)
    col = lambda bi, si: (bi, 0, si)
    const = lambda bi, si: (0, 0)
    shapes = [
        ((b, s, FOX_WIDTH), pl.BlockSpec((None, tm, FOX_WIDTH), row)),
        ((b, s, LANES), pl.BlockSpec((None, tm, LANES), row)),
        ((b, FOX_WIDTH, s), pl.BlockSpec((None, FOX_WIDTH, tm), col)),
        ((b, LANES, s), pl.BlockSpec((None, LANES, tm), col)),
        ((b, s, 2 * FOX_WIDTH), pl.BlockSpec((None, tm, 2 * FOX_WIDTH), row)),
        ((b, s, DIFF_WIDTH), pl.BlockSpec((None, tm, DIFF_WIDTH), row)),
        ((b, DIFF_WIDTH, s), pl.BlockSpec((None, DIFF_WIDTH, tm), col)),
        ((b, s, DIFF_WIDTH), pl.BlockSpec((None, tm, DIFF_WIDTH), row)),
    ]
    return pl.pallas_call(
        _inproj_kernel,
        grid=(b, s // tm),
        in_specs=[
            pl.BlockSpec((None, tm, d), row),
            pl.BlockSpec((1, d), const),
            pl.BlockSpec((d, ncol), const),
            pl.BlockSpec((nkt, d), const),
            pl.BlockSpec((1, LANES), const),
            pl.BlockSpec((tm, tm), const),
        ],
        out_specs=[spec for _, spec in shapes],
        out_shape=[jax.ShapeDtypeStruct(shape, BF16) for shape, _ in shapes],
        scratch_shapes=[pltpu.VMEM((SUBLANES, LANES), F32)],
        compiler_params=pltpu.CompilerParams(
            dimension_semantics=("arbitrary", "arbitrary"), vmem_limit_bytes=VMEM_LIMIT),
        name="inproj",
    )(x, g, w_all, w_kt, bf128, tril)


def _softmax_update(s, m, acc, v_aug):
    m_new = jnp.maximum(m, jnp.max(s, axis=-1, keepdims=True))
    p = jnp.exp(s - m_new).astype(BF16)
    acc_new = jnp.exp(m - m_new) * acc + jnp.dot(p, v_aug, preferred_element_type=F32)
    return m_new, acc_new


def _fox_kernel(q_ref, bq_ref, kt_ref, bkt_ref, v0_ref, v1_ref, o_ref):
    tq = q_ref.shape[0]
    pair = pl.program_id(1)
    qi = pl.program_id(2)
    lane = lax.broadcasted_iota(jnp.int32, (tq, LANES), 1)
    qf = q_ref[...].astype(F32)
    bqf = bq_ref[...].astype(F32)
    qa = []
    for hh in range(2):
        head = 2 * pair + hh
        qm = jnp.where((lane >= FOX_HEAD_DIM * hh) & (lane < FOX_HEAD_DIM * (hh + 1)), qf, 0.0)
        bm = jnp.where(((lane & 7) == head) & (lane < 56), bqf, 0.0)
        qa.append(jnp.concatenate([qm, bm], axis=1).astype(BF16))
    q_all = jnp.concatenate(qa, axis=0)
    v_refs = (v0_ref, v1_ref)

    def step(j, carry, causal):
        ks = pl.multiple_of(j * tq, tq)
        kt = jnp.concatenate([kt_ref[:, pl.ds(ks, tq)], bkt_ref[:, pl.ds(ks, tq)]], axis=0)
        s_all = jnp.dot(q_all, kt, preferred_element_type=F32)
        out = []
        for hh in range(2):
            s = s_all[hh * tq:(hh + 1) * tq]
            if causal:
                row = lax.broadcasted_iota(jnp.int32, s.shape, 0)
                col = lax.broadcasted_iota(jnp.int32, s.shape, 1)
                s = jnp.where(col <= row, s, -jnp.inf)
            m, acc = carry[hh]
            out.append(_softmax_update(s, m, acc, v_refs[hh][pl.ds(ks, tq), :]))
        return tuple(out)

    init = tuple((jnp.full((tq, 1), NEG_BIG, F32), jnp.zeros((tq, LANES), F32)) for _ in range(2))
    def quad(j, c):
        for t in range(4):
            c = step(4 * j + t, c, False)
        return c

    carry = lax.fori_loop(0, qi // 4, quad, init)
    base = (qi // 4) * 4
    carry = lax.cond(qi % 4 >= 2, lambda c: step(base + 1, step(base, c, False), False), lambda c: c, carry)
    (_, acc0), (_, acc1) = lax.cond(qi % 2 == 1, lambda c: step(qi, step(qi - 1, c, False), True),
                                    lambda c: step(qi, c, True), carry)
    o0 = acc0 / pltpu.roll(acc0, FOX_HEAD_DIM, 1)
    o1 = pltpu.roll(acc1, FOX_HEAD_DIM, 1) / acc1
    o_ref[...] = jnp.where(lane < FOX_HEAD_DIM, o0, o1).astype(BF16)


def _fox(qa, bq, kta, bkt, vaa, tq):
    b, s, _ = qa.shape
    npair = FOX_HEADS // 2
    return pl.pallas_call(
        _fox_kernel,
        grid=(b, npair, s // tq),
        in_specs=[
            pl.BlockSpec((None, tq, LANES), lambda bi, p, i: (bi, i, p)),
            pl.BlockSpec((None, tq, LANES), lambda bi, p, i: (bi, i, 0)),
            pl.BlockSpec((None, LANES, s), lambda bi, p, i: (bi, p, 0)),
            pl.BlockSpec((None, LANES, s), lambda bi, p, i: (bi, 0, 0)),
            pl.BlockSpec((None, s, LANES), lambda bi, p, i: (bi, 0, 2 * p)),
            pl.BlockSpec((None, s, LANES), lambda bi, p, i: (bi, 0, 2 * p + 1)),
        ],
        out_specs=pl.BlockSpec((None, tq, LANES), lambda bi, p, i: (bi, i, p)),
        out_shape=jax.ShapeDtypeStruct((b, s, FOX_WIDTH), BF16),
        compiler_params=pltpu.CompilerParams(
            dimension_semantics=("parallel", "parallel", "arbitrary"), vmem_limit_bytes=VMEM_LIMIT),
        name="fox",
    )(qa, bq, kta, bkt, vaa, vaa)


def _diff_kernel(q_ref, pq_ref, kt_ref, pkt_ref, v_ref, slope_ref, lam_ref, g_ref, o_ref, *, out_scale,
                 lambda_init):
    tq = q_ref.shape[0]
    qi = pl.program_id(2)
    lane = lax.broadcasted_iota(jnp.int32, (tq, LANES), 1)
    qf = q_ref[...].astype(F32)
    pq = pq_ref[...]
    q1 = jnp.where(lane < DIFF_HALF_DIM, qf, 0.0).astype(BF16)
    q2 = jnp.where(lane >= DIFF_HALF_DIM, qf, 0.0).astype(BF16)
    q_all = jnp.concatenate([jnp.concatenate([q1, pq], axis=1),
                             jnp.concatenate([q2, pq], axis=1)], axis=0)
    ones = jnp.ones((tq, LANES), BF16)

    def step(j, carry, diagonal):
        ks = pl.multiple_of(j * tq, tq)
        kt = jnp.concatenate([kt_ref[:, pl.ds(ks, tq)], pkt_ref[:, pl.ds(ks, tq)]], axis=0)
        s_all = jnp.dot(q_all, kt, preferred_element_type=F32)
        v_aug = jnp.concatenate([v_ref[pl.ds(ks, tq), :], ones], axis=1)
        if diagonal:
            row = lax.broadcasted_iota(jnp.int32, (tq, tq), 0)
            col = lax.broadcasted_iota(jnp.int32, (tq, tq), 1)
            fix = jnp.where((col // CHUNK) <= (row // CHUNK),
                            (-2.0 * slope_ref[0:1, 0:1]) * jnp.maximum(col - row, 0).astype(F32), -jnp.inf)
        out = []
        for t in range(2):
            s = s_all[t * tq:(t + 1) * tq]
            if diagonal:
                s = s + fix
            m, acc = carry[t]
            out.append(_softmax_update(s, m, acc, v_aug))
        return tuple(out)

    init = tuple((jnp.full((tq, 1), NEG_BIG, F32), jnp.zeros((tq, 2 * LANES), F32)) for _ in range(2))
    carry = lax.fori_loop(0, qi // 2, lambda j, c: step(2 * j + 1, step(2 * j, c, False), False), init)
    (_, acc1), (_, acc2) = lax.cond(qi % 2 == 1, lambda c: step(qi, step(qi - 1, c, False), True),
                                    lambda c: step(qi, c, True), carry)

    lp = lam_ref[...]
    lam = (jnp.exp(jnp.sum(lp[0:1] * lp[1:2], axis=-1, keepdims=True))
           - jnp.exp(jnp.sum(lp[2:3] * lp[3:4], axis=-1, keepdims=True)) + lambda_init)
    o = acc1[:, :LANES] / acc1[:, LANES:] - lam * (acc2[:, :LANES] / acc2[:, LANES:])
    ms = jnp.mean(o * o, axis=-1, keepdims=True)
    o = o * lax.rsqrt(ms + RMS_EPS) * g_ref[...]
    o_ref[...] = (o * out_scale).astype(BF16)


def _diff(qb, posq, ktb, poskt, vb, slopes, lam_rows, subln_g, tq, lambda_init):
    b, s, _ = qb.shape
    kern = functools.partial(_diff_kernel, out_scale=1.0 - lambda_init, lambda_init=lambda_init)
    return pl.pallas_call(
        kern,
        grid=(b, DIFF_HEADS, s // tq),
        in_specs=[
            pl.BlockSpec((None, tq, LANES), lambda bi, h, i: (bi, i, h)),
            pl.BlockSpec((None, tq, LANES), lambda bi, h, i: (h, i, 0)),
            pl.BlockSpec((None, LANES, s), lambda bi, h, i: (bi, h, 0)),
            pl.BlockSpec((LANES, s), lambda bi, h, i: (0, 0)),
            pl.BlockSpec((None, s, LANES), lambda bi, h, i: (bi, 0, h)),
            pl.BlockSpec((None, SUBLANES, LANES), lambda bi, h, i: (h, 0, 0)),
            pl.BlockSpec((4, LANES), lambda bi, h, i: (0, 0)),
            pl.BlockSpec((1, LANES), lambda bi, h, i: (0, 0)),
        ],
        out_specs=pl.BlockSpec((None, tq, LANES), lambda bi, h, i: (bi, i, h)),
        out_shape=jax.ShapeDtypeStruct((b, s, DIFF_WIDTH), BF16),
        compiler_params=pltpu.CompilerParams(
            dimension_semantics=("parallel", "parallel", "arbitrary"), vmem_limit_bytes=VMEM_LIMIT),
        name="diff",
    )(qb, posq, ktb, poskt, vb, slopes, lam_rows, subln_g)


def _alibi_operands(s):
    pos = jnp.arange(s, dtype=jnp.int32)
    hi = (pos // 64).astype(F32)
    lo = (pos % 64).astype(F32)
    slopes = jnp.asarray([2.0 ** (-8.0 * (i + 1) / DIFF_HEADS) for i in range(DIFF_HEADS)], F32)
    sl = slopes[:, None, None]
    lane = jnp.arange(LANES, dtype=jnp.int32)
    qcols = (-sl * 64.0 * hi[None, :, None], -sl * lo[None, :, None], sl * 64.0, sl)
    zq = jnp.zeros((DIFF_HEADS, s, LANES), F32)
    for i, col in enumerate(qcols):
        zq = jnp.where(lane == i, col, zq)
    kcols = (jnp.ones((1, s), F32), jnp.ones((1, s), F32), hi[None], lo[None])
    zk = jnp.zeros((LANES, s), F32)
    for i, col in enumerate(kcols):
        zk = jnp.where(lane[:, None] == i, col, zk)
    slope_tab = jnp.broadcast_to(slopes[:, None, None], (DIFF_HEADS, SUBLANES, LANES))
    return zq.astype(BF16), zk.astype(BF16), slope_tab


def _cmpx(xs, i, j):
    hi = jnp.maximum(xs[i], xs[j])
    lo = jnp.minimum(xs[i], xs[j])
    xs[i], xs[j] = hi, lo


def _bitonic_merge_desc(xs):
    n = len(xs)
    d = n // 2
    while d >= 1:
        for i in range(n):
            if (i & d) == 0:
                _cmpx(xs, i, i + d)
        d //= 2


def _sort_desc(xs):
    n = len(xs)
    k = 2
    while k <= n:
        d = k // 2
        while d >= 1:
            for i in range(n):
                l = i ^ d
                if l > i:
                    if (i & k) == 0:
                        _cmpx(xs, i, l)
                    else:
                        _cmpx(xs, l, i)
            d //= 2
        k *= 2


def _top16_desc(xs):
    xs = list(xs)
    _sort_desc(xs)
    for shift in (4, 2, 1):
        other = [pltpu.roll(xs[15 - i], shift, 0) for i in range(16)]
        xs = [jnp.maximum(xs[i], other[i]) for i in range(16)]
        _bitonic_merge_desc(xs)
    return xs


def _sublane_sum(x):
    for shift in (4, 2, 1):
        x = x + pltpu.roll(x, shift, 0)
    return x


def _route_kernel(oa_ref, ob_ref, x_ref, wo_ref, g_ref, wq_ref, keys_ref,
                  h_ref, hnt_ref, e1_ref, need_ref, e2_ref, rank_ref, sc_ref):
    tm = x_ref.shape[0]
    mixed = jnp.concatenate([oa_ref[...], ob_ref[...]], axis=1)
    h = x_ref[...] + jnp.dot(mixed, wo_ref[...], preferred_element_type=F32)
    h_ref[...] = h
    ms = jnp.mean(h * h, axis=-1, keepdims=True)
    hn_f = h * lax.rsqrt(ms + RMS_EPS) * g_ref[...]
    hn = hn_f.astype(BF16)
    hnt_ref[...] = hn_f.T.astype(BF16)
    q = jnp.dot(hn, wq_ref[...], preferred_element_type=F32).astype(BF16)
    for hp in range(2 * PEER_HEADS):
        qs = q[:, hp * PEER_HALF_DIM:(hp + 1) * PEER_HALF_DIM]
        sc_ref[hp] = lax.dot_general(keys_ref[hp], qs, NT_DIMS, preferred_element_type=F32)

    tw = min(tm, ROUTE_LANES)
    chunks = tm // tw
    sub = lax.broadcasted_iota(jnp.int32, (SUBLANES, tw), 0)
    neg_inf = jnp.full((SUBLANES, tw), -jnp.inf, F32)
    groups = PEER_N_KEYS // SUBLANES

    def spread(vals):
        out = vals[SUBLANES - 1]
        for r in range(SUBLANES - 2, -1, -1):
            out = jnp.where(sub == r, vals[r], out)
        return out

    def head_body(it, _):
        hd = it // chunks
        lanes = pl.ds(pl.multiple_of((it % chunks) * tw, tw), tw)
        s1 = [sc_ref[2 * hd, g * SUBLANES:(g + 1) * SUBLANES, lanes] for g in range(groups)]
        s2 = [sc_ref[2 * hd + 1, g * SUBLANES:(g + 1) * SUBLANES, lanes] for g in range(groups)]
        a = _top16_desc(s1)
        b = _top16_desc(s2)
        b_lo, b_hi, a_hi = spread(b[:8]), spread(b[8:]), spread(a[8:])
        cands = [a[0] + b_lo, a[0] + b_hi, a[1] + b_lo]
        for i, n in ((2, 5), (3, 4), (4, 3), (5, 2), (6, 2), (7, 2)):
            cands.append(jnp.where(sub < n, a[i] + b_lo, neg_inf))
        cands.append(a_hi + b[0])
        top = _top16_desc(cands + [neg_inf] * (16 - len(cands)))
        tau = top[PEER_TOPK - 1]
        z = jnp.zeros((SUBLANES, tw), F32)
        for c in cands:
            z = z + jnp.where(c >= tau, jnp.exp(c - top[0]), 0.0)
        inv_z = 0.5 / _sublane_sum(z)
        for g0 in range(0, groups, 2):
            needs, ranks, e1s, e2s = [], [], [], []
            for g in (g0, g0 + 1):
                need = jnp.full((SUBLANES, tw), PEER_TOPK + 1.0, F32)
                rank = jnp.zeros((SUBLANES, tw), F32)
                for k in range(PEER_TOPK):
                    need = jnp.where(s1[g] + b[k] >= tau, float(PEER_TOPK - k), need)
                for k in reversed(range(PEER_TOPK)):
                    rank = jnp.where(s2[g] >= b[k], float(PEER_TOPK - k), rank)
                needs.append(need)
                ranks.append(rank)
                e1s.append(jnp.exp(s1[g] - a[0]))
                e2s.append(jnp.exp(s2[g] - b[0]) * inv_z)
            rows = pl.ds(g0 * SUBLANES, 2 * SUBLANES)
            e1_ref[hd, rows, lanes] = jnp.concatenate(e1s, axis=0)
            need_ref[hd, rows, lanes] = jnp.concatenate(needs, axis=0)
            e2_ref[hd, rows, lanes] = jnp.concatenate(e2s, axis=0).astype(BF16)
            rank_ref[hd, rows, lanes] = jnp.concatenate(ranks, axis=0).astype(BF16)
        return 0

    lax.fori_loop(0, PEER_HEADS * chunks, head_body, 0)


def _route(oa, ob, x2, wo, g2, wq, keys, tm):
    t, d = x2.shape
    nq = wq.shape[1]
    row = lambda i: (i, 0)
    const2 = lambda i: (0, 0)
    tok3 = lambda i: (0, 0, i)
    score_spec = pl.BlockSpec((PEER_HEADS, PEER_N_KEYS, tm), tok3)
    score_shape = jax.ShapeDtypeStruct((PEER_HEADS, PEER_N_KEYS, t), F32)
    return pl.pallas_call(
        _route_kernel,
        grid=(t // tm,),
        in_specs=[
            pl.BlockSpec((tm, FOX_WIDTH), row),
            pl.BlockSpec((tm, DIFF_WIDTH), row),
            pl.BlockSpec((tm, d), row),
            pl.BlockSpec((d, d), const2),
            pl.BlockSpec((1, d), const2),
            pl.BlockSpec((d, nq), const2),
            pl.BlockSpec((2 * PEER_HEADS, PEER_N_KEYS, PEER_HALF_DIM), lambda i: (0, 0, 0)),
        ],
        out_specs=[
            pl.BlockSpec((tm, d), row),
            pl.BlockSpec((d, tm), lambda i: (0, i)),
            score_spec, score_spec, score_spec, score_spec,
        ],
        out_shape=[
            jax.ShapeDtypeStruct((t, d), F32),
            jax.ShapeDtypeStruct((d, t), BF16),
            score_shape, score_shape,
            jax.ShapeDtypeStruct(score_shape.shape, BF16),
            jax.ShapeDtypeStruct(score_shape.shape, BF16),
        ],
        scratch_shapes=[pltpu.VMEM((2 * PEER_HEADS, PEER_N_KEYS, tm), F32)],
        compiler_params=pltpu.CompilerParams(
            dimension_semantics=("parallel",), vmem_limit_bytes=VMEM_LIMIT),
        name="route",
    )(oa, ob, x2, wo, g2, wq, keys)


def _gelu2(x):
    return x * (1.0 + lax.erf(x * (1.0 / math.sqrt(2.0))))


ROUTE_LANES = 256
MXU_TILE = 256
EW_ROWS = 16
STRIP_TILES = 4


def _peer_kernel(u0_ref, xt0_ref, un_ref, xtn_ref, vtp_ref, e1_ref, need_ref, e2_ref, rank_ref, h_ref, g_ref,
                 o_ref, acc_ref, ht_ref, w_ref, row_ref, *, final_norm, tiles_per_block):
    n = pl.program_id(0)
    rows = un_ref.shape[0]
    tt = xtn_ref.shape[1]
    n_a = rows // PEER_N_KEYS
    col_tiles = tt // MXU_TILE
    assert (rows // MXU_TILE) * col_tiles == n_a

    @pl.when(n == 0)
    def _():
        acc_ref[...] = jnp.zeros_like(acc_ref)
        w_ref[1] = jnp.zeros(w_ref.shape[1:], w_ref.dtype)
        ht_ref[0] = jnp.dot(u0_ref[...], xt0_ref[...], preferred_element_type=F32)

    def run(ht_cur, ht_nxt, w_cur, w_prev, stages):
        for al in stages:
            rt = pl.ds((al // col_tiles) * MXU_TILE, MXU_TILE)
            ct = pl.ds((al % col_tiles) * MXU_TILE, MXU_TILE)
            ht_nxt[rt, ct] = jnp.dot(un_ref[rt, :], xtn_ref[:, ct], preferred_element_type=F32)
            acc_ref[rt, ct] += jnp.dot(vtp_ref[rt, :], w_prev[:, ct], preferred_element_type=F32)
            for hd in range(PEER_HEADS):
                row_ref[0, hd] = jnp.broadcast_to(e1_ref[hd, al:al + 1, :], (EW_ROWS, tt)).astype(BF16)
                row_ref[1, hd] = jnp.broadcast_to(need_ref[hd, al:al + 1, :], (EW_ROWS, tt)).astype(BF16)
            for rc in range(0, PEER_N_KEYS // EW_ROWS, STRIP_TILES):
                tiles = [slice((rc + i) * EW_ROWS, (rc + i + 1) * EW_ROWS) for i in range(STRIP_TILES)]
                w = [None] * STRIP_TILES
                for hd in range(PEER_HEADS):
                    e1t = row_ref[0, hd]
                    needt = row_ref[1, hd]
                    for i, rs in enumerate(tiles):
                        e2t = e2_ref[hd, rs, :]
                        part = e1t * jnp.where(rank_ref[hd, rs, :] >= needt, e2t, jnp.zeros_like(e2t))
                        w[i] = part if w[i] is None else w[i] + part
                for i in range(STRIP_TILES):
                    rd = pl.ds(al * PEER_N_KEYS + (rc + i) * EW_ROWS, EW_ROWS)
                    w_cur[rd, :] = _gelu2(ht_cur[rd, :]).astype(BF16) * w[i]

    cur = n % 2
    run(ht_ref.at[cur], ht_ref.at[1 - cur], w_ref.at[cur], w_ref.at[1 - cur], range(n_a))

    @pl.when((n % tiles_per_block == 0) & (n > 0))
    def _():
        o = acc_ref[...].T + h_ref[...]
        if final_norm:
            ms = jnp.mean(o * o, axis=-1, keepdims=True)
            o = o * lax.rsqrt(ms + RMS_EPS) * g_ref[...]
        o_ref[...] = o
        acc_ref[...] = jnp.zeros_like(acc_ref)


def _peer(u, hnt, vt, e1, need, e2, rank, h, g, tt, rows, final_norm):
    d, t = hnt.shape
    ne = u.shape[0]
    n_a = rows // PEER_N_KEYS
    assert n_a % SUBLANES == 0 and d == rows
    per_block = ne // rows
    n_blocks = t // tt
    n_tiles = per_block * n_blocks
    tok = lambda n: jnp.minimum(n // per_block, n_blocks - 1)
    tok_prev = lambda n: jnp.maximum(n - 1, 0) // per_block
    kern = functools.partial(_peer_kernel, final_norm=final_norm, tiles_per_block=per_block)
    first = pl.BlockSpec((PEER_HEADS, n_a, tt), lambda n: (0, n % per_block, tok(n)))
    second = pl.BlockSpec((PEER_HEADS, PEER_N_KEYS, tt), lambda n: (0, 0, tok(n)))
    return pl.pallas_call(
        kern,
        grid=(n_tiles + 1,),
        in_specs=[
            pl.BlockSpec((rows, d), lambda n: (0, 0)),
            pl.BlockSpec((d, tt), lambda n: (0, 0)),
            pl.BlockSpec((rows, d), lambda n: ((n + 1) % per_block, 0)),
            pl.BlockSpec((d, tt), lambda n: (0, tok(n + 1))),
            pl.BlockSpec((d, rows), lambda n: (0, (n + per_block - 1) % per_block)),
            first, first, second, second,
            pl.BlockSpec((tt, d), lambda n: (tok_prev(n), 0)),
            pl.BlockSpec((1, d), lambda n: (0, 0)),
        ],
        out_specs=pl.BlockSpec((tt, d), lambda n: (tok_prev(n), 0)),
        out_shape=jax.ShapeDtypeStruct((t, d), F32),
        scratch_shapes=[
            pltpu.VMEM((d, tt), F32),
            pltpu.VMEM((2, rows, tt), F32),
            pltpu.VMEM((2, rows, tt), BF16),
            pltpu.VMEM((2, PEER_HEADS, EW_ROWS, tt), BF16),
        ],
        compiler_params=pltpu.CompilerParams(
            dimension_semantics=("arbitrary",), vmem_limit_bytes=VMEM_LIMIT),
        name="peer",
    )(u, hnt, u, hnt, vt, e1, need, e2, rank, h, g)


def _largest_tile(n, cap):
    t = cap
    while n % t:
        t //= 2
    return t


def kernel(x, norm1_g, w_in, b_f, lambda_q1, lambda_k1, lambda_q2, lambda_k2, subln_g, w_out,
           norm2_g, peer_w_q, peer_sub_keys, peer_u, peer_v, final_g):
    b, s, d = x.shape
    depth = w_in.shape[0]
    t = b * s
    tile_s = _largest_tile(s, 512)
    tile_in = _largest_tile(s, 512)
    tile_route = _largest_tile(t, 512)
    tile_peer = _largest_tile(t, 512)
    rows_peer = 1024

    posq, poskt, slope_tab = _alibi_operands(s)
    h = x
    for layer in range(depth):
        w = w_in[layer]
        o = 0
        parts = {}
        for name, width in (("qa", FOX_WIDTH), ("ka", FOX_WIDTH), ("va", FOX_WIDTH), ("f", FOX_HEADS),
                            ("qb", DIFF_WIDTH), ("kb", DIFF_WIDTH), ("vb", DIFF_WIDTH)):
            parts[name] = w[:, o:o + width]
            o += width
        group_of = [g in GATE_GROUPS for g in range(0, LANES, FOX_HEADS)]
        gate = jnp.concatenate([parts["f"] if on else jnp.zeros((d, FOX_HEADS), F32) for on in group_of], axis=1)
        bf128 = jnp.concatenate([b_f[layer].astype(F32) if on else jnp.zeros((FOX_HEADS,), F32)
                                 for on in group_of])[None, :]
        va_pad = jnp.pad(parts["va"].reshape(d, FOX_HEADS, FOX_HEAD_DIM),
                         ((0, 0), (0, 0), (0, FOX_HEAD_DIM))).reshape(d, 2 * FOX_WIDTH)
        w_all = jnp.concatenate([parts["qa"], va_pad, parts["qb"], parts["vb"], gate], axis=1).astype(BF16)
        w_kt = jnp.concatenate([parts["ka"], parts["kb"]], axis=1).T.astype(BF16)

        qa, bq, kta, bkt, vaa, qb, ktb, vb = _inproj(h, norm1_g[layer][None, :], w_all, w_kt, bf128, tile_in)
        out_a = _fox(qa, bq, kta, bkt, vaa, tile_s)

        lambda_init = 0.8 - 0.6 * math.exp(-0.3 * layer)
        lam_rows = jnp.zeros((4, LANES), F32)
        for r, p in enumerate((lambda_q1, lambda_k1, lambda_q2, lambda_k2)):
            lam_rows = lam_rows.at[r, :DIFF_HALF_DIM].set(p[layer].astype(F32))
        out_b = _diff(qb, posq, ktb, poskt, vb, slope_tab, lam_rows, subln_g[layer][None, :].astype(F32),
                      tile_s, lambda_init)

        keys = peer_sub_keys[layer].reshape(2 * PEER_HEADS, PEER_N_KEYS, PEER_HALF_DIM).astype(BF16)
        h2, hnt, e1, need, e2, rank = _route(
            out_a.reshape(t, FOX_WIDTH), out_b.reshape(t, DIFF_WIDTH), h.reshape(t, d),
            w_out[layer].astype(BF16), norm2_g[layer][None, :], peer_w_q[layer].astype(BF16), keys,
            tile_route)

        last = layer == depth - 1
        h = _peer(peer_u[layer].astype(BF16), hnt, peer_v[layer].T.astype(BF16), e1, need, e2, rank,
                  h2, final_g[None, :], tile_peer, rows_peer, last).reshape(b, s, d)
    return h
```
